```python
import math
import jax, jax.numpy as jnp
from jax import lax
import numpy as np

D_MODEL = 4096
BATCH = 2
SEQ = 8192
DEPTH = 2

HEAD_DIM = 128
N_HEADS = D_MODEL // HEAD_DIM
NSA_HEADS = N_HEADS // 2
FOX_HEADS = N_HEADS - NSA_HEADS
NSA_GQA = 4
NSA_KV_HEADS = NSA_HEADS // NSA_GQA
CMP_BLOCK = 32
CMP_STRIDE = 16
SEL_BLOCK = 64
SEL_TOPK = 16
WINDOW = 512
Q_BLOCK = 128
N_BUCKETS = 32
MAX_DISTANCE = 128
D_FF = 256 * ((8 * D_MODEL // 3 + 255) // 256)
N_EXPERTS = 8
TOP_K = 2
D_EXPERT = D_MODEL
N_DENSE = (DEPTH + 1) // 2
N_MOE = DEPTH // 2
FGATE_BIAS_INIT = 4.0
EPS = 1e-6
NEG_INF = -1e30
FORCE_BONUS = 1e4

Q_NSA_W = NSA_HEADS * HEAD_DIM
KV_NSA_W = NSA_KV_HEADS * HEAD_DIM
GATE_W = NSA_HEADS * 3
FOX_W = FOX_HEADS * HEAD_DIM
SPLITS = (Q_NSA_W, KV_NSA_W, KV_NSA_W, KV_NSA_W, KV_NSA_W, KV_NSA_W, KV_NSA_W,
          GATE_W, FOX_W, FOX_W, FOX_W, FOX_HEADS)
IN_COLS = sum(SPLITS)

kernel_name = "hybrid_nsa_fox_moe_trunk"


def rms_norm(x, gain):
    xf = x.astype(jnp.float32)
    y = xf * lax.rsqrt(jnp.mean(xf * xf, axis=-1, keepdims=True) + EPS)
    return (y * gain.astype(jnp.float32)).astype(x.dtype)


def rel_bucket(dist):
    max_exact = N_BUCKETS // 2
    d = jnp.maximum(dist, 0)
    log_ratio = jnp.log(jnp.maximum(d, max_exact).astype(jnp.float32) / max_exact) / math.log(MAX_DISTANCE / max_exact)
    large = jnp.minimum(max_exact + (log_ratio * (N_BUCKETS - max_exact)).astype(jnp.int32), N_BUCKETS - 1)
    return jnp.where(d < max_exact, d, large)


def masked_softmax(s, mask):
    p = jax.nn.softmax(jnp.where(mask, s, NEG_INF), axis=-1)
    return jnp.where(mask, p, 0.0)


def swiglu(h, wg, wu, wd):
    return (jax.nn.silu(h @ wg) * (h @ wu)) @ wd


def compress(kv, pe, w1, w2):
    s = kv.shape[2]
    n_cmp = (s - CMP_BLOCK) // CMP_STRIDE + 1
    idx = jnp.arange(n_cmp)[:, None] * CMP_STRIDE + jnp.arange(CMP_BLOCK)[None, :]
    blocks = kv[:, :, idx] + pe
    flat = blocks.reshape(blocks.shape[0], blocks.shape[1], n_cmp, CMP_BLOCK * HEAD_DIM)
    return jax.nn.silu(flat @ w1) @ w2


def hybrid_mixer(h, w_in, f_bias, cmp_pe, cmp_w1, cmp_w2, rel_bias):
    B, S, _ = h.shape
    G, R, hd = NSA_KV_HEADS, NSA_GQA, HEAD_DIM
    f32 = jnp.float32
    scale = HEAD_DIM ** -0.5
    proj = h @ w_in
    offs = [int(o) for o in np.cumsum(SPLITS)[:-1]]
    (q_n, kc, vc, ks, vs, kw, vw, g_n, q_f, k_f, v_f, f_raw) = jnp.split(proj, offs, axis=-1)

    q_n = q_n.reshape(B, S, G, R, hd).transpose(0, 2, 3, 1, 4) * scale
    to_kv = lambda a: a.reshape(B, S, G, hd).transpose(0, 2, 1, 3)
    kc, vc, ks, vs, kw, vw = (to_kv(a) for a in (kc, vc, ks, vs, kw, vw))
    gates = jax.nn.sigmoid(g_n.reshape(B, S, G, R, 3).transpose(0, 2, 3, 1, 4))
    to_fox = lambda a: a.reshape(B, S, FOX_HEADS, hd).transpose(0, 2, 1, 3)
    q_f = to_fox(q_f) * scale
    k_f = to_fox(k_f)
    v_f = to_fox(v_f)
    log_f = jax.nn.log_sigmoid(f_raw.astype(f32) + f_bias.astype(f32))
    cum = jnp.cumsum(log_f, axis=1).transpose(0, 2, 1)

    k_cmp = compress(kc, cmp_pe[0], cmp_w1[0], cmp_w2[0])
    v_cmp = compress(vc, cmp_pe[1], cmp_w1[1], cmp_w2[1])
    n_cmp = k_cmp.shape[2]
    n_sel = S // SEL_BLOCK
    top_n = min(SEL_TOPK, n_sel)
    n_tok = top_n * SEL_BLOCK
    cmp_end = jnp.arange(n_cmp) * CMP_STRIDE + CMP_BLOCK - 1
    c0 = jnp.arange(n_cmp)[:, None] * CMP_STRIDE
    j0 = jnp.arange(n_sel)[None, :] * SEL_BLOCK
    overlap = jnp.clip(jnp.minimum(c0 + CMP_BLOCK, j0 + SEL_BLOCK) - jnp.maximum(c0, j0), 0).astype(f32) / CMP_BLOCK
    ks_blk = ks.reshape(B, G, n_sel, SEL_BLOCK, hd)
    vs_blk = vs.reshape(B, G, n_sel, SEL_BLOCK, hd)
    kw_pad = jnp.pad(kw, ((0, 0), (0, 0), (WINDOW, 0), (0, 0)))
    vw_pad = jnp.pad(vw, ((0, 0), (0, 0), (WINDOW, 0), (0, 0)))
    tbl = rel_bias.T.reshape(G, R, N_BUCKETS)
    b_idx = jnp.arange(B)[:, None, None, None]
    g_idx = jnp.arange(G)[None, :, None, None]
    g_idx6 = jnp.arange(G)[None, :, None, None, None, None]
    r_idx6 = jnp.arange(R)[None, None, :, None, None, None]
    key_pos = jnp.arange(S)

    def block(i):
        t0 = i * Q_BLOCK
        t = t0 + jnp.arange(Q_BLOCK)
        q = lax.dynamic_slice_in_dim(q_n, t0, Q_BLOCK, axis=3)

        d_c = t[:, None] - cmp_end[None, :]
        s_c = jnp.einsum('bgrqd,bgcd->bgrqc', q, k_cmp).astype(f32) + tbl[:, :, rel_bucket(d_c)]
        p_c = masked_softmax(s_c, d_c >= 0)
        o_c = jnp.einsum('bgrqc,bgcd->bgrqd', p_c.astype(v_cmp.dtype), v_cmp)

        imp = jnp.einsum('bgrqc,cj->bgqj', p_c, overlap)
        blk = jnp.arange(n_sel)[None, :]
        cur = (t // SEL_BLOCK)[:, None]
        forced = (blk == 0) | (blk == cur) | (blk == cur - 1)
        score = jnp.where(blk <= cur, imp + jnp.where(forced, FORCE_BONUS, 0.0), NEG_INF)
        _, sel = lax.top_k(score, top_n)
        k_sel = ks_blk[b_idx, g_idx, sel]
        v_sel = vs_blk[b_idx, g_idx, sel]
        d_s = t[None, None, :, None, None] - (sel[..., None] * SEL_BLOCK + jnp.arange(SEL_BLOCK))
        bias_s = tbl[g_idx6, r_idx6, rel_bucket(d_s)[:, :, None]]
        s_s = jnp.einsum('bgrqd,bgqnld->bgrqnl', q, k_sel).astype(f32) + bias_s
        p_s = masked_softmax(s_s.reshape(B, G, R, Q_BLOCK, n_tok),
                             (d_s >= 0).reshape(B, G, Q_BLOCK, n_tok)[:, :, None])
        o_s = jnp.einsum('bgrqm,bgqmd->bgrqd', p_s.astype(v_sel.dtype),
                         v_sel.reshape(B, G, Q_BLOCK, n_tok, hd))

        k_w = lax.dynamic_slice_in_dim(kw_pad, t0, Q_BLOCK + WINDOW, axis=2)
        v_w = lax.dynamic_slice_in_dim(vw_pad, t0, Q_BLOCK + WINDOW, axis=2)
        pos_w = t0 - WINDOW + jnp.arange(Q_BLOCK + WINDOW)
        d_w = t[:, None] - pos_w[None, :]
        m_w = (d_w >= 0) & (d_w < WINDOW) & (pos_w[None, :] >= 0)
        s_w = jnp.einsum('bgrqd,bgkd->bgrqk', q, k_w).astype(f32) + tbl[:, :, rel_bucket(d_w)]
        p_w = masked_softmax(s_w, m_w)
        o_w = jnp.einsum('bgrqk,bgkd->bgrqd', p_w.astype(v_w.dtype), v_w)

        g = lax.dynamic_slice_in_dim(gates, t0, Q_BLOCK, axis=3)
        o_nsa = g[..., 0:1] * o_c + g[..., 1:2] * o_s + g[..., 2:3] * o_w
        o_nsa = o_nsa.transpose(0, 3, 1, 2, 4).reshape(B, Q_BLOCK, Q_NSA_W)

        qf = lax.dynamic_slice_in_dim(q_f, t0, Q_BLOCK, axis=2)
        cq = lax.dynamic_slice_in_dim(cum, t0, Q_BLOCK, axis=2)
        s_f = jnp.einsum('bhqd,bhkd->bhqk', qf, k_f).astype(f32) + cq[..., None] - cum[:, :, None, :]
        p_f = masked_softmax(s_f, key_pos[None, :] <= t[:, None])
        o_f = jnp.einsum('bhqk,bhkd->bhqd', p_f.astype(v_f.dtype), v_f)
        o_f = o_f.transpose(0, 2, 1, 3).reshape(B, Q_BLOCK, FOX_W)
        return o_nsa, o_f

    o_nsa, o_fox = lax.map(block, jnp.arange(S // Q_BLOCK))
    o_nsa = o_nsa.transpose(1, 0, 2, 3).reshape(B, S, Q_NSA_W)
    o_fox = o_fox.transpose(1, 0, 2, 3).reshape(B, S, FOX_W)
    return o_nsa, o_fox


def moe_swiglu(h, router_w, wg, wu, wd):
    logits = (h @ router_w).astype(jnp.float32)
    top_vals, top_idx = lax.top_k(logits, TOP_K)
    top_w = jax.nn.softmax(top_vals, axis=-1)
    combine = jnp.sum(jax.nn.one_hot(top_idx, N_EXPERTS, dtype=jnp.float32) * top_w[..., None], axis=-2)
    out = jnp.zeros_like(h)
    for e in range(N_EXPERTS):
        out = out + combine[..., e:e + 1].astype(h.dtype) * swiglu(h, wg[e], wu[e], wd[e])
    return out


def setup_inputs(seed: int = 0) -> dict:
    key = jax.random.key(seed)
    ks = jax.random.split(key, 20)
    f32 = jnp.float32
    normal = lambda k, shape, sc: jax.random.normal(k, shape, f32) * sc
    gain = lambda k, shape: 1.0 + normal(k, shape, 0.02)
    return {
        "x": normal(ks[0], (BATCH, SEQ, D_MODEL), 1.0),
        "attn_norm": gain(ks[1], (DEPTH, D_MODEL)),
        "w_in": normal(ks[2], (DEPTH, D_MODEL, IN_COLS), D_MODEL ** -0.5),
        "fgate_bias": FGATE_BIAS_INIT + normal(ks[3], (DEPTH, FOX_HEADS), 0.5),
        "cmp_pe": normal(ks[4], (DEPTH, 2, CMP_BLOCK, HEAD_DIM), 0.02),
        "cmp_w1": normal(ks[5], (DEPTH, 2, CMP_BLOCK * HEAD_DIM, HEAD_DIM), (CMP_BLOCK * HEAD_DIM) ** -0.5),
        "cmp_w2": normal(ks[6], (DEPTH, 2, HEAD_DIM, HEAD_DIM), HEAD_DIM ** -0.5),
        "rel_bias": normal(ks[7], (N_BUCKETS, NSA_HEADS), 0.5),
        "out_norm_nsa": gain(ks[8], (DEPTH, Q_NSA_W)),
        "out_norm_fox": gain(ks[9], (DEPTH, FOX_W)),
        "w_out": normal(ks[10], (DEPTH, D_MODEL, D_MODEL), D_MODEL ** -0.5),
        "ffn_norm": gain(ks[11], (DEPTH, D_MODEL)),
        "dense_w_gate": normal(ks[12], (N_DENSE, D_MODEL, D_FF), D_MODEL ** -0.5),
        "dense_w_up": normal(ks[13], (N_DENSE, D_MODEL, D_FF), D_MODEL ** -0.5),
        "dense_w_down": normal(ks[14], (N_DENSE, D_FF, D_MODEL), D_FF ** -0.5),
        "router_w": normal(ks[15], (N_MOE, D_MODEL, N_EXPERTS), D_MODEL ** -0.5),
        "moe_w_gate": normal(ks[16], (N_MOE, N_EXPERTS, D_MODEL, D_EXPERT), D_MODEL ** -0.5),
        "moe_w_up": normal(ks[17], (N_MOE, N_EXPERTS, D_MODEL, D_EXPERT), D_MODEL ** -0.5),
        "moe_w_down": normal(ks[18], (N_MOE, N_EXPERTS, D_EXPERT, D_MODEL), D_EXPERT ** -0.5),
        "final_norm": gain(ks[19], (D_MODEL,)),
    }


def reference(x, attn_norm, w_in, fgate_bias, cmp_pe, cmp_w1, cmp_w2, rel_bias,
              out_norm_nsa, out_norm_fox, w_out, ffn_norm, dense_w_gate, dense_w_up,
              dense_w_down, router_w, moe_w_gate, moe_w_up, moe_w_down, final_norm):
    for layer in range(DEPTH):
        h = rms_norm(x, attn_norm[layer])
        o_nsa, o_fox = hybrid_mixer(h, w_in[layer], fgate_bias[layer], cmp_pe[layer],
                                    cmp_w1[layer], cmp_w2[layer], rel_bias)
        mixed = jnp.concatenate([rms_norm(o_nsa, out_norm_nsa[layer]),
                                 rms_norm(o_fox, out_norm_fox[layer])], axis=-1)
        x = x + mixed @ w_out[layer]
        h = rms_norm(x, ffn_norm[layer])
        i = layer // 2
        if layer % 2 == 0:
            x = x + swiglu(h, dense_w_gate[i], dense_w_up[i], dense_w_down[i])
        else:
            x = x + moe_swiglu(h, router_w[i], moe_w_gate[i], moe_w_up[i], moe_w_down[i])
    return rms_norm(x, final_norm)
```

```python
import functools
import math

import numpy as np
import jax
import jax.numpy as jnp
from jax import lax
from jax.experimental import pallas as pl
from jax.experimental.pallas import tpu as pltpu

F32 = jnp.float32
BF16 = jnp.bfloat16

HEAD_DIM = 128
NSA_GQA = 4
CMP_BLOCK = 32
CMP_STRIDE = 16
SEL_BLOCK = 64
SEL_TOPK = 16
WINDOW = 512
Q_BLOCK = 128
N_BUCKETS = 32
MAX_DISTANCE = 128
TOP_K = 2
EPS = 1e-6
NEG_INF = -1e30
FORCE_BONUS = 1e4

LANES = 128
VMEM_LIMIT = 56 * 1024 * 1024

MASK_BIG = 2.0 ** 101
REMOVED = -3.0e38
FAR_BUCKET = N_BUCKETS - 1
FAR_DIST = 113
SEL_TK = 512
NEAR = 2 * Q_BLOCK


def _cparams(sem):
    return pltpu.CompilerParams(dimension_semantics=sem, vmem_limit_bytes=VMEM_LIMIT)


def _bucket_np(d):
    max_exact = N_BUCKETS // 2
    dd = np.maximum(d, 0)
    ratio = np.log(np.maximum(dd, max_exact).astype(np.float32) / max_exact) / math.log(MAX_DISTANCE / max_exact)
    large = np.minimum(max_exact + (ratio * (N_BUCKETS - max_exact)).astype(np.int32), N_BUCKETS - 1)
    return np.where(dd < max_exact, dd, large).astype(np.int32)


def _pick(n, prefs):
    for p in prefs:
        if n % p == 0:
            return p
    return n


def _rmsnorm_kernel(x_ref, g_ref, o_ref):
    x = x_ref[...].astype(F32)
    y = x * lax.rsqrt(jnp.mean(x * x, axis=-1, keepdims=True) + EPS)
    o_ref[...] = (y * g_ref[...]).astype(o_ref.dtype)


def _rmsnorm(x, gain, out_dtype):
    t, d = x.shape
    tm = _pick(t, (512, 256, 128))
    return pl.pallas_call(
        _rmsnorm_kernel,
        grid=(t // tm,),
        in_specs=[pl.BlockSpec((tm, d), lambda i: (i, 0)), pl.BlockSpec((1, d), lambda i: (0, 0))],
        out_specs=pl.BlockSpec((tm, d), lambda i: (i, 0)),
        out_shape=jax.ShapeDtypeStruct((t, d), out_dtype),
        compiler_params=_cparams(("parallel",)),
    )(x, gain.reshape(1, d).astype(F32))


def _pair_rmsnorm_kernel(a_ref, b_ref, ga_ref, gb_ref, o_ref):
    wa = a_ref.shape[-1]
    for ref, g, lo in ((a_ref, ga_ref, 0), (b_ref, gb_ref, wa)):
        x = ref[...].astype(F32)
        y = x * lax.rsqrt(jnp.mean(x * x, axis=-1, keepdims=True) + EPS)
        o_ref[:, lo:lo + x.shape[-1]] = (y * g[...]).astype(o_ref.dtype)


def _pair_rmsnorm(a, b, ga, gb):
    t, wa = a.shape
    wb = b.shape[1]
    tm = _pick(t, (512, 256, 128))
    return pl.pallas_call(
        _pair_rmsnorm_kernel,
        grid=(t // tm,),
        in_specs=[pl.BlockSpec((tm, wa), lambda i: (i, 0)), pl.BlockSpec((tm, wb), lambda i: (i, 0)),
                  pl.BlockSpec((1, wa), lambda i: (0, 0)), pl.BlockSpec((1, wb), lambda i: (0, 0))],
        out_specs=pl.BlockSpec((tm, wa + wb), lambda i: (i, 0)),
        out_shape=jax.ShapeDtypeStruct((t, wa + wb), BF16),
        compiler_params=_cparams(("parallel",)),
    )(a, b, ga.reshape(1, wa).astype(F32), gb.reshape(1, wb).astype(F32))


def _mm_kernel(*refs, nk, swiglu, has_cscale, has_res, has_rscale):
    it = iter(refs)
    x_ref = next(it)
    w_ref = next(it)
    w2_ref = next(it) if swiglu else None
    cs_ref = next(it) if has_cscale else None
    res_ref = next(it) if has_res else None
    rs_ref = next(it) if has_rscale else None
    o_ref = next(it)
    acc_ref = next(it) if nk > 1 else None
    acc2_ref = next(it) if (nk > 1 and swiglu) else None

    x = x_ref[...]

    def epilogue(r, r2):
        if swiglu:
            r = (r * jax.nn.sigmoid(r)) * r2
        if has_cscale:
            r = r * cs_ref[...]
        if has_rscale:
            r = r * rs_ref[...]
        if has_res:
            r = res_ref[...] + r
        o_ref[...] = r.astype(o_ref.dtype)

    if nk == 1:
        r = jnp.dot(x, w_ref[...], preferred_element_type=F32)
        r2 = jnp.dot(x, w2_ref[...], preferred_element_type=F32) if swiglu else None
        epilogue(r, r2)
        return

    k = pl.program_id(2)

    def accum(acc, w):
        part = jnp.dot(x, w[...], preferred_element_type=F32)

        @pl.when(k == 0)
        def _():
            acc[...] = part

        @pl.when(k > 0)
        def _():
            acc[...] += part

    accum(acc_ref, w_ref)
    if swiglu:
        accum(acc2_ref, w2_ref)

    @pl.when(k == nk - 1)
    def _():
        epilogue(acc_ref[...], acc2_ref[...] if swiglu else None)


def _matmul(x, w, *, w2=None, cscale=None, res=None, rscale=None, out_dtype=F32,
            tm=1024, tn=512, tk=None):
    m, kdim = x.shape
    n = w.shape[1]
    tm = _pick(m, (tm, 512, 256, 128))
    tn = _pick(n, (tn, 512, 256, 128))
    tk = kdim if tk is None else tk
    assert kdim % tk == 0
    nk = kdim // tk
    swiglu = w2 is not None
    in_specs = [pl.BlockSpec((tm, tk), lambda i, j, k: (i, k)),
                pl.BlockSpec((tk, tn), lambda i, j, k: (k, j))]
    args = [x, w]
    if swiglu:
        in_specs.append(pl.BlockSpec((tk, tn), lambda i, j, k: (k, j)))
        args.append(w2)
    if cscale is not None:
        in_specs.append(pl.BlockSpec((1, tn), lambda i, j, k: (0, j)))
        args.append(cscale)
    if res is not None:
        in_specs.append(pl.BlockSpec((tm, tn), lambda i, j, k: (i, j)))
        args.append(res)
    if rscale is not None:
        in_specs.append(pl.BlockSpec((tm, 1), lambda i, j, k: (i, 0)))
        args.append(rscale)
    scratch = []
    if nk > 1:
        scratch = [pltpu.VMEM((tm, tn), F32)] * (2 if swiglu else 1)
    kern = functools.partial(_mm_kernel, nk=nk, swiglu=swiglu, has_cscale=cscale is not None,
                             has_res=res is not None, has_rscale=rscale is not None)
    return pl.pallas_call(
        kern,
        grid=(m // tm, n // tn, nk),
        in_specs=in_specs,
        out_specs=pl.BlockSpec((tm, tn), lambda i, j, k: (i, j)),
        out_shape=jax.ShapeDtypeStruct((m, n), out_dtype),
        scratch_shapes=scratch,
        compiler_params=_cparams(("parallel", "parallel", "arbitrary")),
    )(*args)


def _cumsum_kernel(f_ref, b_ref, o_ref, carry_ref):
    j = pl.program_id(1)

    @pl.when(j == 0)
    def _():
        carry_ref[...] = jnp.zeros_like(carry_ref)

    z = f_ref[...] + b_ref[...]
    logf = jnp.minimum(z, 0.0) - jnp.log1p(jnp.exp(-jnp.abs(z)))
    ts = z.shape[0]
    row = lax.broadcasted_iota(jnp.int32, (ts, ts), 0)
    col = lax.broadcasted_iota(jnp.int32, (ts, ts), 1)
    tri = jnp.where(col <= row, 1.0, 0.0).astype(F32)
    cum = jnp.dot(tri, logf, preferred_element_type=F32, precision=lax.Precision.HIGHEST)
    cum = cum + carry_ref[0:1, :]
    carry_ref[...] = jnp.broadcast_to(cum[ts - 1:ts, :], carry_ref.shape)
    o_ref[...] = cum.T


def _forget_cumsum(f_raw, f_bias):
    b, s, _ = f_raw.shape
    ts = _pick(s, (256, 128))
    return pl.pallas_call(
        _cumsum_kernel,
        grid=(b, s // ts),
        in_specs=[pl.BlockSpec((None, ts, LANES), lambda bi, j: (bi, j, 0)),
                  pl.BlockSpec((1, LANES), lambda bi, j: (0, 0))],
        out_specs=pl.BlockSpec((None, LANES, ts), lambda bi, j: (bi, 0, j)),
        out_shape=jax.ShapeDtypeStruct((b, LANES, s), F32),
        scratch_shapes=[pltpu.VMEM((8, LANES), F32)],
        compiler_params=_cparams(("parallel", "arbitrary")),
    )(f_raw, f_bias)


_NT = (((1,), (1,)), ((), ()))


def _fox_kernel(q_ref, k_ref, v_ref, cum_ref, o_ref, m_ref, l_ref, acc_ref, *, tq):
    qi = pl.program_id(2)
    q = q_ref[...]
    cq_row = cum_ref[pl.ds(qi, 1), :]
    cq = jnp.broadcast_to(cq_row, (LANES, tq)).T[:, 0:1]
    m_ref[...] = jnp.full_like(m_ref, NEG_INF)
    l_ref[...] = jnp.zeros_like(l_ref)
    acc_ref[...] = jnp.zeros_like(acc_ref)

    def step(ki, diagonal):
        start = pl.multiple_of(ki * tq, tq)
        k = k_ref[pl.ds(start, tq), :]
        v = v_ref[pl.ds(start, tq), :]
        s = lax.dot_general(q, k, _NT, preferred_element_type=F32)
        s = s + cq - cum_ref[pl.ds(ki, 1), :]
        if diagonal:
            row = lax.broadcasted_iota(jnp.int32, s.shape, 0)
            col = lax.broadcasted_iota(jnp.int32, s.shape, 1)
            s = jnp.where(col <= row, s, -jnp.inf)
        m_prev = m_ref[...]
        m_new = jnp.maximum(m_prev, jnp.max(s, axis=1, keepdims=True))
        alpha = jnp.exp(m_prev - m_new)
        p = jnp.exp(s - m_new)
        l_ref[...] = alpha * l_ref[...] + jnp.sum(p, axis=1, keepdims=True)
        acc_ref[...] = alpha * acc_ref[...] + jnp.dot(p.astype(BF16), v, preferred_element_type=F32)
        m_ref[...] = m_new

    def body(ki, carry):
        step(ki, False)
        return carry

    lax.fori_loop(0, qi, body, 0)
    step(qi, True)
    o_ref[...] = acc_ref[...] / l_ref[...]


def _fox_attention(proj3, cum, q0, k0, v0, n_heads):
    b, s, _ = proj3.shape
    tq = _pick(s, (512, 256, 128))
    nq = s // tq
    cum4 = cum.reshape(b, LANES, nq, tq)
    return pl.pallas_call(
        functools.partial(_fox_kernel, tq=tq),
        grid=(b, n_heads, nq),
        in_specs=[pl.BlockSpec((None, tq, HEAD_DIM), lambda bi, h, qi: (bi, qi, q0 + h)),
                  pl.BlockSpec((None, s, HEAD_DIM), lambda bi, h, qi: (bi, 0, k0 + h)),
                  pl.BlockSpec((None, s, HEAD_DIM), lambda bi, h, qi: (bi, 0, v0 + h)),
                  pl.BlockSpec((None, None, nq, tq), lambda bi, h, qi: (bi, h, 0, 0))],
        out_specs=pl.BlockSpec((None, tq, HEAD_DIM), lambda bi, h, qi: (bi, qi, h)),
        out_shape=jax.ShapeDtypeStruct((b, s, n_heads * HEAD_DIM), F32),
        scratch_shapes=[pltpu.VMEM((tq, 1), F32), pltpu.VMEM((tq, 1), F32), pltpu.VMEM((tq, HEAD_DIM), F32)],
        compiler_params=_cparams(("parallel", "parallel", "arbitrary")),
    )(proj3, proj3, proj3, cum4)


def _compress_kernel(r_ref, w1_ref, pe_ref, w2_ref, o_ref):
    ab = jnp.dot(r_ref[...], w1_ref[...], preferred_element_type=F32)
    nc = ab.shape[0]
    first = ab[:, :HEAD_DIM]
    second = pltpu.roll(ab[:, HEAD_DIM:], nc - 1, axis=0)
    pe_term = jnp.dot(pe_ref[:, :pe_ref.shape[1] // 2], w1_ref[:, :HEAD_DIM], preferred_element_type=F32) + \
        jnp.dot(pe_ref[:, pe_ref.shape[1] // 2:], w1_ref[:, HEAD_DIM:], preferred_element_type=F32)
    pre = first + second + pe_term[0:1, :]
    hid = pre * jax.nn.sigmoid(pre)
    o_ref[...] = jnp.dot(hid.astype(BF16), w2_ref[...], preferred_element_type=F32).astype(o_ref.dtype)


def _compress(rows, w1cat, pe_flat, w2):
    two, b, g, nc, width = rows.shape
    return pl.pallas_call(
        _compress_kernel,
        grid=(two, b, g),
        in_specs=[pl.BlockSpec((None, None, None, nc, width), lambda a, bi, gi: (a, bi, gi, 0, 0)),
                  pl.BlockSpec((None, width, 2 * HEAD_DIM), lambda a, bi, gi: (a, 0, 0)),
                  pl.BlockSpec((None, 8, 2 * width), lambda a, bi, gi: (a, 0, 0)),
                  pl.BlockSpec((None, HEAD_DIM, HEAD_DIM), lambda a, bi, gi: (a, 0, 0))],
        out_specs=pl.BlockSpec((None, None, None, nc, HEAD_DIM), lambda a, bi, gi: (a, bi, gi, 0, 0)),
        out_shape=jax.ShapeDtypeStruct((two, b, g, nc, HEAD_DIM), BF16),
        compiler_params=_cparams(("parallel", "parallel", "parallel")),
    )(rows, w1cat, pe_flat, w2)


def _stack_heads(q_ref):
    return jnp.concatenate([q_ref[:, r * HEAD_DIM:(r + 1) * HEAD_DIM] for r in range(NSA_GQA)], axis=0)


def _cmp_attn_kernel(q_ref, kc_ref, vc_ref, gd_ref, ov_ref, oc_ref, ns_ref, *, n_sel, top_n):
    i = pl.program_id(2)
    t0 = i * Q_BLOCK
    q4 = _stack_heads(q_ref)
    s = lax.dot_general(q4, kc_ref[...], _NT, preferred_element_type=F32)
    rows, ncp = s.shape
    shift = lax.rem(i * (Q_BLOCK // CMP_STRIDE) + ncp // 2, ncp)
    bias = pltpu.roll(gd_ref[...], shift, axis=1)
    t = t0 + (lax.broadcasted_iota(jnp.int32, s.shape, 0) & (Q_BLOCK - 1))
    cmp_end = lax.broadcasted_iota(jnp.int32, s.shape, 1) * CMP_STRIDE + (CMP_BLOCK - 1)
    s = jnp.where(t >= cmp_end, s + bias, -jnp.inf)
    m = jnp.maximum(jnp.max(s, axis=1, keepdims=True), NEG_INF)
    p = jnp.exp(s - m)
    l = jnp.sum(p, axis=1, keepdims=True)
    p = (p * jnp.where(l > 0.0, 1.0 / l, 0.0)).astype(BF16)
    o = jnp.dot(p, vc_ref[...], preferred_element_type=F32)
    for r in range(NSA_GQA):
        oc_ref[:, r * HEAD_DIM:(r + 1) * HEAD_DIM] = o[r * Q_BLOCK:(r + 1) * Q_BLOCK]
    imp4 = jnp.dot(p, ov_ref[...], preferred_element_type=F32)
    imp = imp4[0:Q_BLOCK]
    for r in range(1, NSA_GQA):
        imp = imp + imp4[r * Q_BLOCK:(r + 1) * Q_BLOCK]

    shape = (Q_BLOCK, LANES)
    lane = lax.broadcasted_iota(jnp.int32, shape, 1).astype(F32)
    cur = ((t0 + lax.broadcasted_iota(jnp.int32, shape, 0)) // SEL_BLOCK).astype(F32)
    visible = lane <= cur
    forced = (lane == 0.0) | (lane == cur) | (lane == cur - 1.0)
    score = jnp.where(visible, imp + jnp.where(forced, FORCE_BONUS, 0.0), NEG_INF)
    score = jnp.where(lane < float(n_sel), score, REMOVED)

    def pick(_, carry):
        score, sel = carry
        mx = jnp.max(score, axis=1, keepdims=True)
        first = jnp.min(jnp.where(score == mx, lane, float(LANES)), axis=1, keepdims=True)
        hit = lane == first
        return jnp.where(hit, REMOVED, score), jnp.where(hit, 1.0, sel)

    _, sel = lax.fori_loop(0, top_n, pick, (score, jnp.zeros(shape, F32)))
    ns_ref[...] = jnp.where((sel > 0.0) & visible, 0.0, -MASK_BIG).astype(ns_ref.dtype)


def _cmp_attention(proj3, kv_cmp, gd, overlap, n_groups):
    b, s, _ = proj3.shape
    ncp = kv_cmp.shape[3]
    n_sel = s // SEL_BLOCK
    width = NSA_GQA * HEAD_DIM
    kern = functools.partial(_cmp_attn_kernel, n_sel=n_sel, top_n=min(SEL_TOPK, n_sel))
    return pl.pallas_call(
        kern,
        grid=(b, n_groups, s // Q_BLOCK),
        in_specs=[pl.BlockSpec((None, Q_BLOCK, width), lambda bi, g, i: (bi, i, g)),
                  pl.BlockSpec((None, None, None, ncp, HEAD_DIM), lambda bi, g, i: (0, bi, g, 0, 0)),
                  pl.BlockSpec((None, None, None, ncp, HEAD_DIM), lambda bi, g, i: (1, bi, g, 0, 0)),
                  pl.BlockSpec((None, NSA_GQA * Q_BLOCK, ncp), lambda bi, g, i: (g, 0, 0)),
                  pl.BlockSpec((ncp, LANES), lambda bi, g, i: (0, 0))],
        out_specs=[pl.BlockSpec((None, Q_BLOCK, width), lambda bi, g, i: (bi, i, g)),
                   pl.BlockSpec((None, None, Q_BLOCK, LANES), lambda bi, g, i: (bi, g, i, 0))],
        out_shape=[jax.ShapeDtypeStruct((b, s, n_groups * width), F32),
                   jax.ShapeDtypeStruct((b, n_groups, s, LANES), BF16)],
        compiler_params=_cparams(("parallel", "parallel", "parallel")),
    )(proj3, kv_cmp, kv_cmp, gd, overlap)


def _sel_win_kernel(q_ref, ka_ref, vs_ref, kw_ref, vw_ref, ns_ref, oc_ref, g_ref, bn_ref, bw_ref,
                    o_ref, m_ref, l_ref, acc_ref):
    i = pl.program_id(2)
    t0 = pl.multiple_of(i * Q_BLOCK, Q_BLOCK)
    q4 = _stack_heads(q_ref)
    rows = q4.shape[0]

    ns = ns_ref[...]
    lane = lax.broadcasted_iota(jnp.int32, ns.shape, 1)
    near_blk = (lane >= 2 * i - 2) & (lane <= 2 * i + 1)
    ns_far = jnp.where(near_blk, -MASK_BIG, ns.astype(F32)).astype(BF16)
    qa_far = jnp.concatenate([q4, jnp.concatenate([ns_far] * NSA_GQA, axis=0)], axis=1)
    qa_near = jnp.concatenate([q4, jnp.concatenate([ns] * NSA_GQA, axis=0)], axis=1)

    m_ref[...] = jnp.full_like(m_ref, NEG_INF)
    l_ref[...] = jnp.zeros_like(l_ref)
    acc_ref[...] = jnp.zeros_like(acc_ref)

    def online(s, v):
        m_prev = m_ref[...]
        m_new = jnp.maximum(m_prev, jnp.max(s, axis=1, keepdims=True))
        alpha = jnp.exp(m_prev - m_new)
        p = jnp.exp(s - m_new)
        l_ref[...] = alpha * l_ref[...] + jnp.sum(p, axis=1, keepdims=True)
        acc_ref[...] = alpha * acc_ref[...] + jnp.dot(p.astype(BF16), v, preferred_element_type=F32)
        m_ref[...] = m_new

    def far_step(j, carry):
        start = pl.multiple_of(Q_BLOCK + j * SEL_TK, Q_BLOCK)
        s = lax.dot_general(qa_far, ka_ref[pl.ds(start, SEL_TK), :], _NT, preferred_element_type=F32)
        online(s, vs_ref[pl.ds(start, SEL_TK), :])
        return carry

    n_far = jnp.where(i == 0, 0, (i + 2) // 4)
    lax.fori_loop(0, n_far, far_step, 0)

    s = lax.dot_general(qa_near, ka_ref[pl.ds(t0, NEAR), :], _NT, preferred_element_type=F32)
    s = s + bn_ref[...]
    key_pos = t0 - Q_BLOCK + lax.broadcasted_iota(jnp.int32, s.shape, 1)
    s = jnp.where(key_pos >= 0, s, -jnp.inf)
    online(s, vs_ref[pl.ds(t0, NEAR), :])
    o_sel = acc_ref[...] / l_ref[...]

    span = Q_BLOCK + WINDOW
    sw = lax.dot_general(q4, kw_ref[pl.ds(t0, span), :], _NT, preferred_element_type=F32)
    sw = sw + bw_ref[...]
    key_pos = t0 - WINDOW + lax.broadcasted_iota(jnp.int32, sw.shape, 1)
    sw = jnp.where(key_pos >= 0, sw, -jnp.inf)
    mw = jnp.max(sw, axis=1, keepdims=True)
    pw = jnp.exp(sw - mw)
    lw = jnp.sum(pw, axis=1, keepdims=True)
    o_win = jnp.dot(pw.astype(BF16), vw_ref[pl.ds(t0, span), :], preferred_element_type=F32) / lw

    gate = jax.nn.sigmoid(g_ref[...])
    for r in range(NSA_GQA):
        lo, hi = r * Q_BLOCK, (r + 1) * Q_BLOCK
        o = (gate[:, 3 * r:3 * r + 1] * oc_ref[:, r * HEAD_DIM:(r + 1) * HEAD_DIM]
             + gate[:, 3 * r + 1:3 * r + 2] * o_sel[lo:hi]
             + gate[:, 3 * r + 2:3 * r + 3] * o_win[lo:hi])
        o_ref[:, r * HEAD_DIM:(r + 1) * HEAD_DIM] = o


def _sel_win_attention(proj3, k_aug, v_sel, k_win, v_win, negsel, o_cmp, gates, bias_near, bias_win, n_groups):
    b, s, _ = proj3.shape
    width = NSA_GQA * HEAD_DIM
    rows = NSA_GQA * Q_BLOCK
    sp_sel = k_aug.shape[1]
    sp_win = k_win.shape[1]
    return pl.pallas_call(
        _sel_win_kernel,
        grid=(b, n_groups, s // Q_BLOCK),
        in_specs=[pl.BlockSpec((None, Q_BLOCK, width), lambda bi, g, i: (bi, i, g)),
                  pl.BlockSpec((None, sp_sel, 2 * HEAD_DIM), lambda bi, g, i: (bi, 0, g)),
                  pl.BlockSpec((None, sp_sel, HEAD_DIM), lambda bi, g, i: (bi, 0, g)),
                  pl.BlockSpec((None, sp_win, HEAD_DIM), lambda bi, g, i: (bi, 0, g)),
                  pl.BlockSpec((None, sp_win, HEAD_DIM), lambda bi, g, i: (bi, 0, g)),
                  pl.BlockSpec((None, None, Q_BLOCK, LANES), lambda bi, g, i: (bi, g, i, 0)),
                  pl.BlockSpec((None, Q_BLOCK, width), lambda bi, g, i: (bi, i, g)),
                  pl.BlockSpec((None, Q_BLOCK, LANES), lambda bi, g, i: (bi, i, g)),
                  pl.BlockSpec((None, rows, NEAR), lambda bi, g, i: (g, 0, 0)),
                  pl.BlockSpec((None, rows, Q_BLOCK + WINDOW), lambda bi, g, i: (g, 0, 0))],
        out_specs=pl.BlockSpec((None, Q_BLOCK, width), lambda bi, g, i: (bi, i, g)),
        out_shape=jax.ShapeDtypeStruct((b, s, n_groups * width), F32),
        scratch_shapes=[pltpu.VMEM((rows, 1), F32), pltpu.VMEM((rows, 1), F32), pltpu.VMEM((rows, HEAD_DIM), F32)],
        compiler_params=_cparams(("parallel", "parallel", "arbitrary")),
    )(proj3, k_aug, v_sel, k_win, v_win, negsel, o_cmp, gates, bias_near, bias_win)


def _router_kernel(h_ref, w_ref, o_ref, *, n_experts):
    logits = jnp.dot(h_ref[...], w_ref[...], preferred_element_type=F32)
    lane = lax.broadcasted_iota(jnp.int32, logits.shape, 1).astype(F32)
    logits = jnp.where(lane < float(n_experts), logits, -jnp.inf)
    m1 = jnp.max(logits, axis=1, keepdims=True)
    i1 = jnp.min(jnp.where(logits == m1, lane, float(LANES)), axis=1, keepdims=True)
    hit1 = lane == i1
    rest = jnp.where(hit1, -jnp.inf, logits)
    m2 = jnp.max(rest, axis=1, keepdims=True)
    i2 = jnp.min(jnp.where(rest == m2, lane, float(LANES)), axis=1, keepdims=True)
    hit2 = lane == i2
    e2 = jnp.exp(m2 - m1)
    denom = 1.0 + e2
    o_ref[...] = jnp.where(hit1, 1.0 / denom, 0.0) + jnp.where(hit2, e2 / denom, 0.0)


def _router(h, w_router_padded, n_experts):
    t, d = h.shape
    tm = _pick(t, (512, 256, 128))
    return pl.pallas_call(
        functools.partial(_router_kernel, n_experts=n_experts),
        grid=(t // tm,),
        in_specs=[pl.BlockSpec((tm, d), lambda i: (i, 0)), pl.BlockSpec((d, LANES), lambda i: (0, 0))],
        out_specs=pl.BlockSpec((tm, LANES), lambda i: (i, 0)),
        out_shape=jax.ShapeDtypeStruct((t, LANES), F32),
        compiler_params=_cparams(("parallel",)),
    )(h, w_router_padded)


def _static_tables(s):
    n_sel = s // SEL_BLOCK
    ncp = max(LANES, -(-(s // CMP_STRIDE) // LANES) * LANES)
    n_cmp = (s - CMP_BLOCK) // CMP_STRIDE + 1
    r = np.arange(Q_BLOCK)[:, None]
    d_near = r + Q_BLOCK - np.arange(NEAR)[None, :]
    d_win = r + WINDOW - np.arange(Q_BLOCK + WINDOW)[None, :]
    d_cmp = r - CMP_STRIDE * (np.arange(ncp)[None, :] - ncp // 2) - (CMP_BLOCK - 1)
    c0 = np.arange(ncp)[:, None] * CMP_STRIDE
    j0 = np.arange(LANES)[None, :] * SEL_BLOCK
    overlap = np.clip(np.minimum(c0 + CMP_BLOCK, j0 + SEL_BLOCK) - np.maximum(c0, j0), 0, None).astype(np.float32) / CMP_BLOCK
    overlap[n_cmp:, :] = 0.0
    overlap[:, n_sel:] = 0.0
    onehot = (np.arange(s)[:, None] // SEL_BLOCK == np.arange(LANES)[None, :]).astype(np.float32)
    return dict(
        ncp=ncp,
        b_near=_bucket_np(d_near), ok_near=d_near >= 0,
        b_win=_bucket_np(d_win), ok_win=(d_win >= 0) & (d_win < WINDOW),
        b_cmp=_bucket_np(d_cmp), ok_cmp=(d_cmp >= 0) & (d_cmp < FAR_DIST),
        overlap=overlap, onehot=onehot)


def _bias_tables(rel_bias, tabs, n_groups):
    tbl = rel_bias.T.astype(F32)
    reb = tbl - tbl[:, FAR_BUCKET:FAR_BUCKET + 1]

    def tile(bucket, ok, masked_value):
        vals = jnp.where(jnp.asarray(ok)[None], reb[:, jnp.asarray(bucket)], masked_value)
        return vals.reshape(n_groups, NSA_GQA * Q_BLOCK, vals.shape[-1])

    return (tile(tabs["b_near"], tabs["ok_near"], -MASK_BIG),
            tile(tabs["b_win"], tabs["ok_win"], -MASK_BIG),
            tile(tabs["b_cmp"], tabs["ok_cmp"], 0.0))


def _mixer(h, w_in, f_bias, cmp_pe, cmp_w1, cmp_w2, biases, tabs, b, s):
    d = h.shape[1]
    n_heads = d // HEAD_DIM
    hn = n_heads // 2
    hf = n_heads - hn
    g = hn // NSA_GQA
    gw = g * HEAD_DIM
    qn_w, fox_w = hn * HEAD_DIM, hf * HEAD_DIM
    splits = (qn_w, gw, gw, gw, gw, gw, gw, hn * 3, fox_w, fox_w, fox_w, hf)
    offs = np.concatenate([[0], np.cumsum(splits)])
    col = lambda k: w_in[:, offs[k]:offs[k + 1]]
    scale = HEAD_DIM ** -0.5

    w_main = jnp.concatenate([col(k) for k in (0, 1, 2, 3, 4, 5, 6, 8, 9, 10)], axis=1).astype(BF16)
    blocks = np.cumsum([0, hn, g, g, g, g, g, g, hf, hf, hf])
    cscale = np.ones((1, w_main.shape[1]), np.float32)
    cscale[0, :qn_w] = scale
    cscale[0, blocks[7] * HEAD_DIM:blocks[8] * HEAD_DIM] = scale
    proj = _matmul(h, w_main, cscale=jnp.asarray(cscale), out_dtype=BF16)
    proj3 = proj.reshape(b, s, -1)

    w_gate = jnp.pad(col(7).reshape(d, g, NSA_GQA * 3), ((0, 0), (0, 0), (0, LANES - NSA_GQA * 3))).reshape(d, g * LANES)
    w_f = jnp.pad(col(11), ((0, 0), (0, LANES - hf)))
    small = _matmul(h, jnp.concatenate([w_gate, w_f], axis=1).astype(BF16), tn=(g + 1) * LANES)
    small3 = small.reshape(b, s, -1)
    gates = small3[:, :, :g * LANES]
    f_raw = small3[:, :, g * LANES:]

    cum = _forget_cumsum(f_raw, jnp.pad(f_bias.astype(F32), (0, LANES - hf)).reshape(1, LANES))
    o_fox = _fox_attention(proj3, cum, int(blocks[7]), int(blocks[8]), int(blocks[9]), hf)

    ncp = tabs["ncp"]
    grp = lambda k: proj3[:, :, blocks[k] * HEAD_DIM:blocks[k + 1] * HEAD_DIM]

    def to_rows(a):
        a = a.reshape(b, s // CMP_STRIDE, CMP_STRIDE, g, HEAD_DIM).transpose(0, 3, 1, 2, 4)
        a = a.reshape(b, g, s // CMP_STRIDE, CMP_STRIDE * HEAD_DIM)
        return jnp.pad(a, ((0, 0), (0, 0), (0, ncp - s // CMP_STRIDE), (0, 0)))

    rows = jnp.stack([to_rows(grp(1)), to_rows(grp(2))])
    half = CMP_STRIDE * HEAD_DIM
    w1cat = jnp.concatenate([cmp_w1[:, :half], cmp_w1[:, half:]], axis=2).astype(BF16)
    pe_flat = jnp.pad(cmp_pe.reshape(2, 1, CMP_BLOCK * HEAD_DIM), ((0, 0), (0, 7), (0, 0))).astype(BF16)
    kv_cmp = _compress(rows, w1cat, pe_flat, cmp_w2.astype(BF16))

    bias_near, bias_win, bias_cmp = biases
    o_cmp, negsel = _cmp_attention(proj3, kv_cmp, bias_cmp, jnp.asarray(tabs["overlap"], BF16), g)

    onehot = jnp.broadcast_to(jnp.asarray(tabs["onehot"], BF16)[None, :, None, :], (b, s, g, LANES))
    k_aug = jnp.concatenate([grp(3).reshape(b, s, g, HEAD_DIM), onehot], axis=-1).reshape(b, s, g * 2 * HEAD_DIM)
    front = lambda a, n: jnp.pad(a, ((0, 0), (n, 0), (0, 0)))
    o_nsa = _sel_win_attention(proj3, front(k_aug, Q_BLOCK), front(grp(4), Q_BLOCK),
                               front(grp(5), WINDOW), front(grp(6), WINDOW),
                               negsel, o_cmp, gates, bias_near, bias_win, g)
    return o_nsa.reshape(b * s, qn_w), o_fox.reshape(b * s, fox_w)


def _pad_to(a, axis, mult):
    n = a.shape[axis]
    target = -(-n // mult) * mult
    if target == n:
        return a
    pad = [(0, 0)] * a.ndim
    pad[axis] = (0, target - n)
    return jnp.pad(a, pad)


def _dense_ffn(x, h, wg, wu, wd):
    ff_tile = 512
    wg = _pad_to(wg, 1, ff_tile).astype(BF16)
    wu = _pad_to(wu, 1, ff_tile).astype(BF16)
    wd = _pad_to(wd, 0, ff_tile).astype(BF16)
    act = _matmul(h, wg, w2=wu, out_dtype=BF16, tn=ff_tile)
    return _matmul(act, wd, res=x, tm=1024, tn=1024, tk=wd.shape[0] // 4)


def _moe_ffn(x, h, router_w, wg, wu, wd):
    n_experts = router_w.shape[1]
    combine = _router(h, jnp.pad(router_w, ((0, 0), (0, LANES - n_experts))).astype(BF16), n_experts)
    out = x
    for e in range(n_experts):
        act = _matmul(h, wg[e].astype(BF16), w2=wu[e].astype(BF16), out_dtype=BF16)
        out = _matmul(act, wd[e].astype(BF16), res=out, rscale=combine[:, e:e + 1])
    return out


def kernel(x, attn_norm, w_in, fgate_bias, cmp_pe, cmp_w1, cmp_w2, rel_bias, out_norm_nsa, out_norm_fox,
           w_out, ffn_norm, dense_w_gate, dense_w_up, dense_w_down, router_w, moe_w_gate, moe_w_up,
           moe_w_down, final_norm):
    b, s, d = x.shape
    depth = attn_norm.shape[0]
    n_groups = (d // HEAD_DIM // 2) // NSA_GQA
    assert s % SEL_TK == 0 and s // SEL_BLOCK <= LANES and d % (2 * NSA_GQA * HEAD_DIM) == 0
    tabs = _static_tables(s)
    biases = _bias_tables(rel_bias, tabs, n_groups)
    xt = x.reshape(b * s, d)
    for layer in range(depth):
        h = _rmsnorm(xt, attn_norm[layer], BF16)
        o_nsa, o_fox = _mixer(h, w_in[layer], fgate_bias[layer], cmp_pe[layer], cmp_w1[layer], cmp_w2[layer],
                              biases, tabs, b, s)
        mixed = _pair_rmsnorm(o_nsa, o_fox, out_norm_nsa[layer], out_norm_fox[layer])
        xt = _matmul(mixed, w_out[layer].astype(BF16), res=xt)
        h = _rmsnorm(xt, ffn_norm[layer], BF16)
        i = layer // 2
        if layer % 2 == 0:
            xt = _dense_ffn(xt, h, dense_w_gate[i], dense_w_up[i], dense_w_down[i])
        else:
            xt = _moe_ffn(xt, h, router_w[i], moe_w_gate[i], moe_w_up[i], moe_w_down[i])
    return _rmsnorm(xt, final_norm, x.dtype).reshape(b, s, d)
```

```python
import functools
import math

import numpy as np
import jax
import jax.numpy as jnp
from jax import lax
from jax.experimental import pallas as pl
from jax.experimental.pallas import tpu as pltpu

F32 = jnp.float32
BF16 = jnp.bfloat16

HEAD_DIM = 128
NSA_GQA = 4
CMP_BLOCK = 32
CMP_STRIDE = 16
SEL_BLOCK = 64
SEL_TOPK = 16
WINDOW = 512
Q_BLOCK = 128
N_BUCKETS = 32
MAX_DISTANCE = 128
TOP_K = 2
EPS = 1e-6
NEG_INF = -1e30
FORCE_BONUS = 1e4

LANES = 128
VMEM_LIMIT = 56 * 1024 * 1024

LOG2E = 1.4426950408889634
MASK_BIG = 2.0 ** 101
REMOVED = -3.0e38
FAR_BUCKET = N_BUCKETS - 1
FAR_DIST = 113
SEL_TK = 512
NEAR = 2 * Q_BLOCK
CMP_NEAR = (-9, 7)
MOE_TM = 512
GATHER_ROWS = 512

_NT = (((1,), (1,)), ((), ()))


def _cparams(sem):
    return pltpu.CompilerParams(dimension_semantics=sem, vmem_limit_bytes=VMEM_LIMIT)


def _bucket_np(d):
    max_exact = N_BUCKETS // 2
    dd = np.maximum(d, 0)
    ratio = np.log(np.maximum(dd, max_exact).astype(np.float32) / max_exact) / math.log(MAX_DISTANCE / max_exact)
    large = np.minimum(max_exact + (ratio * (N_BUCKETS - max_exact)).astype(np.int32), N_BUCKETS - 1)
    return np.where(dd < max_exact, dd, large).astype(np.int32)


def _pick(n, prefs):
    for p in prefs:
        if n % p == 0:
            return p
    return n


def _rmsnorm_kernel(x_ref, g_ref, o_ref):
    x = x_ref[...].astype(F32)
    y = x * lax.rsqrt(jnp.mean(x * x, axis=-1, keepdims=True) + EPS)
    o_ref[...] = (y * g_ref[...]).astype(o_ref.dtype)


def _rmsnorm(x, gain, out_dtype):
    t, d = x.shape
    tm = _pick(t, (512, 256, 128))
    return pl.pallas_call(
        _rmsnorm_kernel,
        grid=(t // tm,),
        in_specs=[pl.BlockSpec((tm, d), lambda i: (i, 0)), pl.BlockSpec((1, d), lambda i: (0, 0))],
        out_specs=pl.BlockSpec((tm, d), lambda i: (i, 0)),
        out_shape=jax.ShapeDtypeStruct((t, d), out_dtype),
        compiler_params=_cparams(("parallel",)),
    )(x, gain.reshape(1, d).astype(F32))


def _pair_rmsnorm_kernel(a_ref, b_ref, ga_ref, gb_ref, o_ref):
    wa = a_ref.shape[-1]
    for ref, g, lo in ((a_ref, ga_ref, 0), (b_ref, gb_ref, wa)):
        x = ref[...].astype(F32)
        y = x * lax.rsqrt(jnp.mean(x * x, axis=-1, keepdims=True) + EPS)
        o_ref[:, lo:lo + x.shape[-1]] = (y * g[...]).astype(o_ref.dtype)


def _pair_rmsnorm(a, b, ga, gb):
    t, wa = a.shape
    wb = b.shape[1]
    tm = _pick(t, (512, 256, 128))
    return pl.pallas_call(
        _pair_rmsnorm_kernel,
        grid=(t // tm,),
        in_specs=[pl.BlockSpec((tm, wa), lambda i: (i, 0)), pl.BlockSpec((tm, wb), lambda i: (i, 0)),
                  pl.BlockSpec((1, wa), lambda i: (0, 0)), pl.BlockSpec((1, wb), lambda i: (0, 0))],
        out_specs=pl.BlockSpec((tm, wa + wb), lambda i: (i, 0)),
        out_shape=jax.ShapeDtypeStruct((t, wa + wb), BF16),
        compiler_params=_cparams(("parallel",)),
    )(a, b, ga.reshape(1, wa).astype(F32), gb.reshape(1, wb).astype(F32))


def _mm_kernel(*refs, nk, swiglu, has_cscale, has_res):
    it = iter(refs)
    x_ref = next(it)
    w_ref = next(it)
    w2_ref = next(it) if swiglu else None
    cs_ref = next(it) if has_cscale else None
    res_ref = next(it) if has_res else None
    o_ref = next(it)
    acc_ref = next(it) if nk > 1 else None
    acc2_ref = next(it) if (nk > 1 and swiglu) else None

    x = x_ref[...]

    def epilogue(r, r2):
        if swiglu:
            r = (r * jax.nn.sigmoid(r)) * r2
        if has_cscale:
            r = r * cs_ref[...]
        if has_res:
            r = res_ref[...] + r
        o_ref[...] = r.astype(o_ref.dtype)

    if nk == 1:
        r = jnp.dot(x, w_ref[...], preferred_element_type=F32)
        r2 = jnp.dot(x, w2_ref[...], preferred_element_type=F32) if swiglu else None
        epilogue(r, r2)
        return

    k = pl.program_id(2)

    def accum(acc, w):
        part = jnp.dot(x, w[...], preferred_element_type=F32)

        @pl.when(k == 0)
        def _():
            acc[...] = part

        @pl.when(k > 0)
        def _():
            acc[...] += part

    accum(acc_ref, w_ref)
    if swiglu:
        accum(acc2_ref, w2_ref)

    @pl.when(k == nk - 1)
    def _():
        epilogue(acc_ref[...], acc2_ref[...] if swiglu else None)


def _matmul(x, w, *, w2=None, cscale=None, res=None, out_dtype=F32, tm=1024, tn=512, tk=None):
    m, kdim = x.shape
    n = w.shape[1]
    tm = _pick(m, (tm, 512, 256, 128))
    tn = _pick(n, (tn, 512, 256, 128))
    tk = kdim if tk is None else tk
    assert kdim % tk == 0
    nk = kdim // tk
    swiglu = w2 is not None
    in_specs = [pl.BlockSpec((tm, tk), lambda i, j, k: (i, k)),
                pl.BlockSpec((tk, tn), lambda i, j, k: (k, j))]
    args = [x, w]
    if swiglu:
        in_specs.append(pl.BlockSpec((tk, tn), lambda i, j, k: (k, j)))
        args.append(w2)
    if cscale is not None:
        in_specs.append(pl.BlockSpec((1, tn), lambda i, j, k: (0, j)))
        args.append(cscale)
    if res is not None:
        in_specs.append(pl.BlockSpec((tm, tn), lambda i, j, k: (i, j)))
        args.append(res)
    scratch = []
    if nk > 1:
        scratch = [pltpu.VMEM((tm, tn), F32)] * (2 if swiglu else 1)
    kern = functools.partial(_mm_kernel, nk=nk, swiglu=swiglu, has_cscale=cscale is not None,
                             has_res=res is not None)
    return pl.pallas_call(
        kern,
        grid=(m // tm, n // tn, nk),
        in_specs=in_specs,
        out_specs=pl.BlockSpec((tm, tn), lambda i, j, k: (i, j)),
        out_shape=jax.ShapeDtypeStruct((m, n), out_dtype),
        scratch_shapes=scratch,
        compiler_params=_cparams(("parallel", "parallel", "arbitrary")),
    )(*args)


def _cumsum_kernel(f_ref, b_ref, o_ref, carry_ref):
    j = pl.program_id(1)

    @pl.when(j == 0)
    def _():
        carry_ref[...] = jnp.zeros_like(carry_ref)

    z = f_ref[...] + b_ref[...]
    logf = jnp.minimum(z, 0.0) - jnp.log1p(jnp.exp(-jnp.abs(z)))
    ts = z.shape[0]
    row = lax.broadcasted_iota(jnp.int32, (ts, ts), 0)
    col = lax.broadcasted_iota(jnp.int32, (ts, ts), 1)
    tri = jnp.where(col <= row, 1.0, 0.0).astype(F32)
    cum = jnp.dot(tri, logf, preferred_element_type=F32, precision=lax.Precision.HIGHEST)
    cum = cum + carry_ref[0:1, :]
    carry_ref[...] = jnp.broadcast_to(cum[ts - 1:ts, :], carry_ref.shape)
    o_ref[...] = cum


def _forget_cumsum(f_raw, f_bias):
    b, s, _ = f_raw.shape
    ts = _pick(s, (256, 128))
    return pl.pallas_call(
        _cumsum_kernel,
        grid=(b, s // ts),
        in_specs=[pl.BlockSpec((None, ts, LANES), lambda bi, j: (bi, j, 0)),
                  pl.BlockSpec((1, LANES), lambda bi, j: (0, 0))],
        out_specs=pl.BlockSpec((None, ts, LANES), lambda bi, j: (bi, j, 0)),
        out_shape=jax.ShapeDtypeStruct((b, s, LANES), F32),
        scratch_shapes=[pltpu.VMEM((8, LANES), F32)],
        compiler_params=_cparams(("parallel", "arbitrary")),
    )(f_raw, f_bias)


def _online_step(st, vt, m_ref, l_ref, acc_ref):
    m_prev = m_ref[...]
    m_new = jnp.maximum(m_prev, jnp.max(st, axis=0, keepdims=True))
    alpha = jnp.exp2(m_prev - m_new)
    p = jnp.exp2(st - m_new)
    l_ref[...] = alpha * l_ref[...] + jnp.sum(p, axis=0, keepdims=True)
    acc_ref[...] = alpha * acc_ref[...] + jnp.dot(vt, p.astype(BF16), preferred_element_type=F32)
    m_ref[...] = m_new


def _softmax_init(m_ref, l_ref, acc_ref):
    m_ref[...] = jnp.full_like(m_ref, NEG_INF)
    l_ref[...] = jnp.zeros_like(l_ref)
    acc_ref[...] = jnp.zeros_like(acc_ref)


def _lane_concat(blocks, n):
    return blocks[0] if n == 1 else jnp.concatenate([blocks[j] for j in range(n)], axis=1)


def _sweep_tiles(n_tiles, step):
    def pair(j, carry):
        step(2 * j, 2)
        return carry

    lax.fori_loop(0, n_tiles // 2, pair, 0)

    @pl.when(n_tiles % 2 == 1)
    def _():
        step(n_tiles - 1, 1)


def _fox_kernel(q_ref, k_ref, vt_ref, c_ref, o_ref, m_ref, l_ref, acc_ref, ck_ref, *, tq):
    qi = pl.program_id(2)
    nq = c_ref.shape[0]

    @pl.when(qi == 0)
    def _():
        for j in range(nq):
            ck_ref[j * tq:(j + 1) * tq, :] = jnp.broadcast_to(c_ref[j:j + 1, :], (LANES, tq)).T

    q = q_ref[...]
    cq = c_ref[pl.ds(qi, 1), :]
    _softmax_init(m_ref, l_ref, acc_ref)

    def step(ki, count, diagonal=False):
        start = pl.multiple_of(ki * tq, tq)
        st = lax.dot_general(k_ref[pl.ds(start, count * tq), :], q, _NT,
                             preferred_element_type=F32)
        ck = ck_ref[pl.ds(start, count * tq), :]
        st = st + cq - jnp.concatenate([ck] * (tq // LANES), axis=1)
        if diagonal:
            key = lax.broadcasted_iota(jnp.int32, st.shape, 0)
            qry = lax.broadcasted_iota(jnp.int32, st.shape, 1)
            st = jnp.where(key <= qry, st, -jnp.inf)
        _online_step(st, _lane_concat(vt_ref[pl.ds(ki, count)], count), m_ref, l_ref, acc_ref)

    _sweep_tiles(qi, step)
    step(qi, 1, diagonal=True)
    o_ref[...] = (acc_ref[...] / l_ref[...]).T


def _fox_attention(proj3, q0, k0, v, cum):
    b, s, h, hd = v.shape
    tq = _pick(s, (512, 256, 128))
    nq = s // tq
    c2 = (cum * LOG2E).transpose(0, 2, 1).reshape(b, h, nq, tq)
    vt = v.reshape(b, nq, tq, h, hd).transpose(0, 3, 1, 4, 2)
    return pl.pallas_call(
        functools.partial(_fox_kernel, tq=tq),
        grid=(b, h, nq),
        in_specs=[pl.BlockSpec((None, tq, hd), lambda bi, hi, qi: (bi, qi, q0 + hi)),
                  pl.BlockSpec((None, s, hd), lambda bi, hi, qi: (bi, 0, k0 + hi)),
                  pl.BlockSpec((None, None, nq, hd, tq), lambda bi, hi, qi: (bi, hi, 0, 0, 0)),
                  pl.BlockSpec((None, None, nq, tq), lambda bi, hi, qi: (bi, hi, 0, 0))],
        out_specs=pl.BlockSpec((None, tq, hd), lambda bi, hi, qi: (bi, qi, hi)),
        out_shape=jax.ShapeDtypeStruct((b, s, h * hd), F32),
        scratch_shapes=[pltpu.VMEM((1, tq), F32), pltpu.VMEM((1, tq), F32), pltpu.VMEM((hd, tq), F32),
                        pltpu.VMEM((s, LANES), F32)],
        compiler_params=_cparams(("parallel", "parallel", "arbitrary")),
    )(proj3, proj3, vt, c2)


def _compress_kernel(r_ref, w1_ref, pe_ref, w2_ref, o_ref):
    ab = jnp.dot(r_ref[...], w1_ref[...], preferred_element_type=F32)
    nc = ab.shape[0]
    half = pe_ref.shape[1] // 2
    first = ab[:, :HEAD_DIM]
    second = pltpu.roll(ab[:, HEAD_DIM:], nc - 1, axis=0)
    pe_term = (jnp.dot(pe_ref[:, :half], w1_ref[:, :HEAD_DIM], preferred_element_type=F32)
               + jnp.dot(pe_ref[:, half:], w1_ref[:, HEAD_DIM:], preferred_element_type=F32))
    pre = first + second + pe_term[0:1, :]
    hid = pre * jax.nn.sigmoid(pre)
    o_ref[...] = jnp.dot(hid.astype(BF16), w2_ref[...], preferred_element_type=F32).astype(o_ref.dtype)


def _compress(rows, w1cat, pe_flat, w2):
    two, b, g, nc, width = rows.shape
    return pl.pallas_call(
        _compress_kernel,
        grid=(two, b, g),
        in_specs=[pl.BlockSpec((None, None, None, nc, width), lambda a, bi, gi: (a, bi, gi, 0, 0)),
                  pl.BlockSpec((None, width, 2 * HEAD_DIM), lambda a, bi, gi: (a, 0, 0)),
                  pl.BlockSpec((None, 8, 2 * width), lambda a, bi, gi: (a, 0, 0)),
                  pl.BlockSpec((None, HEAD_DIM, HEAD_DIM), lambda a, bi, gi: (a, 0, 0))],
        out_specs=pl.BlockSpec((None, None, None, nc, HEAD_DIM), lambda a, bi, gi: (a, bi, gi, 0, 0)),
        out_shape=jax.ShapeDtypeStruct((two, b, g, nc, HEAD_DIM), BF16),
        compiler_params=_cparams(("parallel", "parallel", "parallel")),
    )(rows, w1cat, pe_flat, w2)


def _stack_heads(q_ref):
    return jnp.concatenate([q_ref[:, r * HEAD_DIM:(r + 1) * HEAD_DIM] for r in range(NSA_GQA)], axis=0)


def _cmp_attn_kernel(q_ref, kc_ref, vc_ref, gd_ref, ov_ref, oc_ref, ns_ref, *, n_sel, top_n):
    i = pl.program_id(2)
    t0 = i * Q_BLOCK
    q4 = _stack_heads(q_ref)
    s = lax.dot_general(q4, kc_ref[...], _NT, preferred_element_type=F32)
    ncp = s.shape[1]
    shift = lax.rem(i * (Q_BLOCK // CMP_STRIDE) + ncp // 2, ncp)
    bias = pltpu.roll(gd_ref[...], shift, axis=1)
    t = t0 + (lax.broadcasted_iota(jnp.int32, s.shape, 0) & (Q_BLOCK - 1))
    cmp_end = lax.broadcasted_iota(jnp.int32, s.shape, 1) * CMP_STRIDE + (CMP_BLOCK - 1)
    s = jnp.where(t >= cmp_end, s + bias, -jnp.inf)
    m = jnp.maximum(jnp.max(s, axis=1, keepdims=True), NEG_INF)
    p = jnp.exp2(s - m)
    l = jnp.sum(p, axis=1, keepdims=True)
    p = (p * jnp.where(l > 0.0, 1.0 / l, 0.0)).astype(BF16)
    o = jnp.dot(p, vc_ref[...], preferred_element_type=F32)
    for r in range(NSA_GQA):
        oc_ref[:, r * HEAD_DIM:(r + 1) * HEAD_DIM] = o[r * Q_BLOCK:(r + 1) * Q_BLOCK]
    imp4 = jnp.dot(p, ov_ref[...], preferred_element_type=F32)
    imp = imp4[0:Q_BLOCK]
    for r in range(1, NSA_GQA):
        imp = imp + imp4[r * Q_BLOCK:(r + 1) * Q_BLOCK]

    shape = (Q_BLOCK, LANES)
    lane = lax.broadcasted_iota(jnp.int32, shape, 1).astype(F32)
    cur = ((t0 + lax.broadcasted_iota(jnp.int32, shape, 0)) // SEL_BLOCK).astype(F32)
    visible = lane <= cur
    forced = (lane == 0.0) | (lane == cur) | (lane == cur - 1.0)
    score = jnp.where(visible, imp + jnp.where(forced, FORCE_BONUS, 0.0), NEG_INF)
    score = jnp.where(lane < float(n_sel), score, REMOVED)

    def pick(_, carry):
        score, sel = carry
        mx = jnp.max(score, axis=1, keepdims=True)
        first = jnp.min(jnp.where(score == mx, lane, float(LANES)), axis=1, keepdims=True)
        hit = lane == first
        return jnp.where(hit, REMOVED, score), jnp.where(hit, 1.0, sel)

    _, sel = lax.fori_loop(0, top_n, pick, (score, jnp.zeros(shape, F32)))
    ns_ref[...] = jnp.where((sel > 0.0) & visible, 0.0, -MASK_BIG).astype(ns_ref.dtype)


def _cmp_attention(proj3, kv_cmp, gd, overlap, n_groups):
    b, s, _ = proj3.shape
    ncp = kv_cmp.shape[3]
    n_sel = s // SEL_BLOCK
    width = NSA_GQA * HEAD_DIM
    kern = functools.partial(_cmp_attn_kernel, n_sel=n_sel, top_n=min(SEL_TOPK, n_sel))
    return pl.pallas_call(
        kern,
        grid=(b, n_groups, s // Q_BLOCK),
        in_specs=[pl.BlockSpec((None, Q_BLOCK, width), lambda bi, g, i: (bi, i, g)),
                  pl.BlockSpec((None, None, None, ncp, HEAD_DIM), lambda bi, g, i: (0, bi, g, 0, 0)),
                  pl.BlockSpec((None, None, None, ncp, HEAD_DIM), lambda bi, g, i: (1, bi, g, 0, 0)),
                  pl.BlockSpec((None, NSA_GQA * Q_BLOCK, ncp), lambda bi, g, i: (g, 0, 0)),
                  pl.BlockSpec((ncp, LANES), lambda bi, g, i: (0, 0))],
        out_specs=[pl.BlockSpec((None, Q_BLOCK, width), lambda bi, g, i: (bi, i, g)),
                   pl.BlockSpec((None, None, Q_BLOCK, LANES), lambda bi, g, i: (bi, g, i, 0))],
        out_shape=[jax.ShapeDtypeStruct((b, s, n_groups * width), F32),
                   jax.ShapeDtypeStruct((b, n_groups, s, LANES), BF16)],
        compiler_params=_cparams(("parallel", "parallel", "parallel")),
    )(proj3, kv_cmp, kv_cmp, gd, overlap)


def _sel_win_kernel(q_ref, ka_ref, vst_ref, kw_ref, vwt_ref, ns_ref, oc_ref, g_ref, bn_ref, bw_ref,
                    o_ref, m_ref, l_ref, acc_ref):
    i = pl.program_id(2)
    t0 = pl.multiple_of(i * Q_BLOCK, Q_BLOCK)
    q4 = _stack_heads(q_ref)

    ns = ns_ref[...]
    lane = lax.broadcasted_iota(jnp.int32, ns.shape, 1)
    near_blk = (lane >= 2 * i - 2) & (lane <= 2 * i + 1)
    ns_far = jnp.where(near_blk, -MASK_BIG, ns.astype(F32)).astype(BF16)
    qa_far = jnp.concatenate([q4, jnp.concatenate([ns_far] * NSA_GQA, axis=0)], axis=1)
    qa_near = jnp.concatenate([q4, jnp.concatenate([ns] * NSA_GQA, axis=0)], axis=1)
    _softmax_init(m_ref, l_ref, acc_ref)

    blocks_per_tile = SEL_TK // Q_BLOCK

    def far_step(j, count):
        start = pl.multiple_of(Q_BLOCK + j * SEL_TK, Q_BLOCK)
        st = lax.dot_general(ka_ref[pl.ds(start, count * SEL_TK), :], qa_far, _NT, preferred_element_type=F32)
        n_blk = count * blocks_per_tile
        vt = _lane_concat(vst_ref[pl.ds(1 + j * blocks_per_tile, n_blk)], n_blk)
        _online_step(st, vt, m_ref, l_ref, acc_ref)

    _sweep_tiles(jnp.where(i == 0, 0, (i + 2) // 4), far_step)

    st = lax.dot_general(ka_ref[pl.ds(t0, NEAR), :], qa_near, _NT, preferred_element_type=F32)
    st = st + bn_ref[...]
    key_pos = t0 - Q_BLOCK + lax.broadcasted_iota(jnp.int32, st.shape, 0)
    st = jnp.where(key_pos >= 0, st, -jnp.inf)
    _online_step(st, _lane_concat(vst_ref[pl.ds(i, 2)], 2), m_ref, l_ref, acc_ref)
    o_sel = acc_ref[...] / l_ref[...]

    span = Q_BLOCK + WINDOW
    sw = lax.dot_general(kw_ref[pl.ds(t0, span), :], q4, _NT, preferred_element_type=F32)
    sw = sw + bw_ref[...]
    key_pos = t0 - WINDOW + lax.broadcasted_iota(jnp.int32, sw.shape, 0)
    sw = jnp.where(key_pos >= 0, sw, -jnp.inf)
    pw = jnp.exp2(sw - jnp.max(sw, axis=0, keepdims=True))
    lw = jnp.sum(pw, axis=0, keepdims=True)
    n_blk = span // Q_BLOCK
    o_win = jnp.dot(_lane_concat(vwt_ref[pl.ds(i, n_blk)], n_blk), pw.astype(BF16),
                    preferred_element_type=F32) / lw

    gate = jax.nn.sigmoid(g_ref[...])
    gate_t = gate.T
    for r in range(NSA_GQA):
        lo, hi = r * Q_BLOCK, (r + 1) * Q_BLOCK
        mix_t = gate_t[3 * r + 1:3 * r + 2, :] * o_sel[:, lo:hi] + gate_t[3 * r + 2:3 * r + 3, :] * o_win[:, lo:hi]
        o_ref[:, lo:hi] = gate[:, 3 * r:3 * r + 1] * oc_ref[:, lo:hi] + mix_t.T


def _sel_win_attention(proj3, k_aug, vt_sel, k_win, vt_win, negsel, o_cmp, gates, bias_near, bias_win, n_groups):
    b, s, _ = proj3.shape
    width = NSA_GQA * HEAD_DIM
    rows = NSA_GQA * Q_BLOCK
    sp_sel = k_aug.shape[1]
    sp_win = k_win.shape[1]
    nb_sel = vt_sel.shape[2]
    nb_win = vt_win.shape[2]
    return pl.pallas_call(
        _sel_win_kernel,
        grid=(b, n_groups, s // Q_BLOCK),
        in_specs=[pl.BlockSpec((None, Q_BLOCK, width), lambda bi, g, i: (bi, i, g)),
                  pl.BlockSpec((None, sp_sel, 2 * HEAD_DIM), lambda bi, g, i: (bi, 0, g)),
                  pl.BlockSpec((None, None, nb_sel, HEAD_DIM, Q_BLOCK), lambda bi, g, i: (bi, g, 0, 0, 0)),
                  pl.BlockSpec((None, sp_win, HEAD_DIM), lambda bi, g, i: (bi, 0, g)),
                  pl.BlockSpec((None, None, nb_win, HEAD_DIM, Q_BLOCK), lambda bi, g, i: (bi, g, 0, 0, 0)),
                  pl.BlockSpec((None, None, Q_BLOCK, LANES), lambda bi, g, i: (bi, g, i, 0)),
                  pl.BlockSpec((None, Q_BLOCK, width), lambda bi, g, i: (bi, i, g)),
                  pl.BlockSpec((None, Q_BLOCK, LANES), lambda bi, g, i: (bi, i, g)),
                  pl.BlockSpec((None, NEAR, rows), lambda bi, g, i: (g, 0, 0)),
                  pl.BlockSpec((None, Q_BLOCK + WINDOW, rows), lambda bi, g, i: (g, 0, 0))],
        out_specs=pl.BlockSpec((None, Q_BLOCK, width), lambda bi, g, i: (bi, i, g)),
        out_shape=jax.ShapeDtypeStruct((b, s, n_groups * width), F32),
        scratch_shapes=[pltpu.VMEM((1, rows), F32), pltpu.VMEM((1, rows), F32), pltpu.VMEM((HEAD_DIM, rows), F32)],
        compiler_params=_cparams(("parallel", "parallel", "arbitrary")),
    )(proj3, k_aug, vt_sel, k_win, vt_win, negsel, o_cmp, gates, bias_near, bias_win)


def _router_kernel(h_ref, w_ref, o_ref, *, n_experts):
    logits = jnp.dot(h_ref[...], w_ref[...], preferred_element_type=F32)
    lane = lax.broadcasted_iota(jnp.int32, logits.shape, 1).astype(F32)
    logits = jnp.where(lane < float(n_experts), logits, -jnp.inf)
    m1 = jnp.max(logits, axis=1, keepdims=True)
    i1 = jnp.min(jnp.where(logits == m1, lane, float(LANES)), axis=1, keepdims=True)
    rest = jnp.where(lane == i1, -jnp.inf, logits)
    m2 = jnp.max(rest, axis=1, keepdims=True)
    i2 = jnp.min(jnp.where(rest == m2, lane, float(LANES)), axis=1, keepdims=True)
    e2 = jnp.exp(m2 - m1)
    denom = 1.0 + e2
    o_ref[...] = (jnp.where(lane == 0.0, i1, 0.0) + jnp.where(lane == 1.0, i2, 0.0)
                  + jnp.where(lane == 2.0, 1.0 / denom, 0.0) + jnp.where(lane == 3.0, e2 / denom, 0.0))


def _router(h, w_router_padded, n_experts):
    t, d = h.shape
    tm = _pick(t, (512, 256, 128))
    return pl.pallas_call(
        functools.partial(_router_kernel, n_experts=n_experts),
        grid=(t // tm,),
        in_specs=[pl.BlockSpec((tm, d), lambda i: (i, 0)), pl.BlockSpec((d, LANES), lambda i: (0, 0))],
        out_specs=pl.BlockSpec((tm, LANES), lambda i: (i, 0)),
        out_shape=jax.ShapeDtypeStruct((t, LANES), F32),
        compiler_params=_cparams(("parallel",)),
    )(h, w_router_padded)


def _row_copy(src_ref, dst_ref, sem, src_row, dst_row):
    return pltpu.make_async_copy(src_ref.at[pl.ds(src_row, 1)], dst_ref.at[pl.ds(dst_row, 1)], sem)


def _gather_rows_kernel(idx_ref, src_ref, out_ref, sem, *, rows):
    base = pl.program_id(0) * rows

    def start(r, carry):
        _row_copy(src_ref, out_ref, sem, idx_ref[base + r], base + r).start()
        return carry

    lax.fori_loop(0, rows, start, 0)

    def wait(r, carry):
        _row_copy(src_ref, out_ref, sem, 0, base + r).wait()
        return carry

    lax.fori_loop(0, rows, wait, 0)


def _gather_rows(src, idx):
    n_out = idx.shape[0]
    n_src, width = src.shape
    rows = _pick(n_out, (GATHER_ROWS, 256, 128))
    src = src.reshape(n_src, width // LANES, LANES)
    out = pl.pallas_call(
        functools.partial(_gather_rows_kernel, rows=rows),
        grid_spec=pltpu.PrefetchScalarGridSpec(
            num_scalar_prefetch=1,
            grid=(n_out // rows,),
            in_specs=[pl.BlockSpec(memory_space=pl.ANY)],
            out_specs=pl.BlockSpec(memory_space=pl.ANY),
            scratch_shapes=[pltpu.SemaphoreType.DMA(())]),
        out_shape=jax.ShapeDtypeStruct((n_out,) + src.shape[1:], src.dtype),
        compiler_params=_cparams(("arbitrary",)),
    )(idx, src)
    return out.reshape(n_out, width)


def _gmm_kernel(te_ref, nv_ref, x_ref, w_ref, *rest, swiglu):
    o_ref = rest[-1]
    live = pl.program_id(0) < nv_ref[0]

    @pl.when(live)
    def _():
        x = x_ref[...]
        r = jnp.dot(x, w_ref[...], preferred_element_type=F32)
        if swiglu:
            r = (r * jax.nn.sigmoid(r)) * jnp.dot(x, rest[0][...], preferred_element_type=F32)
        o_ref[...] = r.astype(o_ref.dtype)

    @pl.when(jnp.logical_not(live))
    def _():
        o_ref[...] = jnp.zeros_like(o_ref)


def _grouped_matmul(x, w, tile_expert, n_valid, *, w2=None, tn=512):
    p, kdim = x.shape
    n = w.shape[2]
    tn = _pick(n, (tn, 256, 128))
    wspec = pl.BlockSpec((None, kdim, tn), lambda i, j, te, nv: (te[i], 0, j))
    in_specs = [pl.BlockSpec((MOE_TM, kdim), lambda i, j, te, nv: (i, 0)), wspec]
    args = [x, w]
    if w2 is not None:
        in_specs.append(wspec)
        args.append(w2)
    return pl.pallas_call(
        functools.partial(_gmm_kernel, swiglu=w2 is not None),
        grid_spec=pltpu.PrefetchScalarGridSpec(
            num_scalar_prefetch=2,
            grid=(p // MOE_TM, n // tn),
            in_specs=in_specs,
            out_specs=pl.BlockSpec((MOE_TM, tn), lambda i, j, te, nv: (i, j))),
        out_shape=jax.ShapeDtypeStruct((p, n), BF16),
        compiler_params=_cparams(("parallel", "parallel")),
    )(tile_expert, n_valid, *args)


def _combine_kernel(x_ref, ya_ref, yb_ref, r_ref, o_ref):
    route = r_ref[...]
    o_ref[...] = (x_ref[...] + route[:, 2:3] * ya_ref[...].astype(F32)
                  + route[:, 3:4] * yb_ref[...].astype(F32))


def _moe_combine(x, y2, route):
    t, d = x.shape
    tm = _pick(t, (256, 128))
    nt = t // tm
    return pl.pallas_call(
        _combine_kernel,
        grid=(nt,),
        in_specs=[pl.BlockSpec((tm, d), lambda i: (i, 0)),
                  pl.BlockSpec((tm, d), lambda i: (i, 0)),
                  pl.BlockSpec((tm, d), lambda i: (i + nt, 0)),
                  pl.BlockSpec((tm, LANES), lambda i: (i, 0))],
        out_specs=pl.BlockSpec((tm, d), lambda i: (i, 0)),
        out_shape=jax.ShapeDtypeStruct((t, d), F32),
        compiler_params=_cparams(("parallel",)),
    )(x, y2, y2, route)


def _route_plan(i1, i2, n_experts):
    t = i1.shape[0]
    e_flat = jnp.concatenate([i1, i2])
    onehot = (e_flat[:, None] == jnp.arange(n_experts, dtype=jnp.int32)[None, :]).astype(jnp.int32)
    csum = jnp.cumsum(onehot, axis=0)
    counts = csum[-1]
    rank = jnp.sum((csum - onehot) * onehot, axis=1)
    padded = ((counts + MOE_TM - 1) // MOE_TM) * MOE_TM
    ends = jnp.cumsum(padded)
    pos = jnp.sum(onehot * (ends - padded)[None, :], axis=1) + rank
    p_rows = 2 * t + n_experts * MOE_TM
    token = jnp.tile(jnp.arange(t, dtype=jnp.int32), 2)
    src = jnp.zeros((p_rows,), jnp.int32).at[pos].set(token)
    tile_start = jnp.arange(p_rows // MOE_TM, dtype=jnp.int32) * MOE_TM
    tile_expert = jnp.minimum(jnp.sum((tile_start[:, None] >= ends[None, :]).astype(jnp.int32), axis=1),
                              n_experts - 1)
    n_valid = (ends[-1] // MOE_TM).reshape(1)
    return pos.astype(jnp.int32), src, tile_expert.astype(jnp.int32), n_valid.astype(jnp.int32)


def _moe_ffn(x, h, router_w, wg, wu, wd):
    n_experts = router_w.shape[1]
    route = _router(h, jnp.pad(router_w, ((0, 0), (0, LANES - n_experts))).astype(BF16), n_experts)
    i1 = route[:, 0].astype(jnp.int32)
    i2 = route[:, 1].astype(jnp.int32)
    pos, src, tile_expert, n_valid = _route_plan(i1, i2, n_experts)
    xs = _gather_rows(h, src)
    act = _grouped_matmul(xs, wg.astype(BF16), tile_expert, n_valid, w2=wu.astype(BF16))
    y = _grouped_matmul(act, wd.astype(BF16), tile_expert, n_valid)
    return _moe_combine(x, _gather_rows(y, pos), route)


def _static_tables(s):
    n_sel = s // SEL_BLOCK
    ncp = max(LANES, -(-(s // CMP_STRIDE) // LANES) * LANES)
    n_cmp = (s - CMP_BLOCK) // CMP_STRIDE + 1
    span = Q_BLOCK + WINDOW
    d_near = (NEAR - 1) - np.arange(NEAR + Q_BLOCK - 1)
    d_win = (span - 1) - np.arange(span + Q_BLOCK - 1)
    r = np.arange(Q_BLOCK)[:, None]
    d_cmp = r - CMP_STRIDE * np.arange(CMP_NEAR[0], CMP_NEAR[1])[None, :] - (CMP_BLOCK - 1)
    c0 = np.arange(ncp)[:, None] * CMP_STRIDE
    j0 = np.arange(LANES)[None, :] * SEL_BLOCK
    overlap = np.clip(np.minimum(c0 + CMP_BLOCK, j0 + SEL_BLOCK) - np.maximum(c0, j0), 0, None).astype(np.float32) / CMP_BLOCK
    overlap[n_cmp:, :] = 0.0
    overlap[:, n_sel:] = 0.0
    onehot = (np.arange(s)[:, None] // SEL_BLOCK == np.arange(LANES)[None, :]).astype(np.float32)
    return dict(
        ncp=ncp,
        b_near=_bucket_np(d_near), ok_near=d_near >= 0,
        b_win=_bucket_np(d_win), ok_win=(d_win >= 0) & (d_win < WINDOW),
        b_cmp=_bucket_np(d_cmp), ok_cmp=(d_cmp >= 0) & (d_cmp < FAR_DIST),
        overlap=overlap, onehot=onehot)


def _bias_tables(rel_bias, tabs, n_groups):
    tbl = rel_bias.T.astype(F32) * LOG2E
    reb = tbl - tbl[:, FAR_BUCKET:FAR_BUCKET + 1]
    heads = reb.shape[0]

    def toeplitz(bucket, ok, width):
        seq = jnp.where(jnp.asarray(ok)[None, :], reb[:, jnp.asarray(bucket)], -MASK_BIG)
        tile = jnp.stack([seq[:, Q_BLOCK - 1 - r:Q_BLOCK - 1 - r + width] for r in range(Q_BLOCK)], axis=1)
        tile = tile.reshape(n_groups, NSA_GQA, Q_BLOCK, width).transpose(0, 3, 1, 2)
        return tile.reshape(n_groups, width, NSA_GQA * Q_BLOCK)

    ncp = tabs["ncp"]
    win = jnp.where(jnp.asarray(tabs["ok_cmp"])[None], reb[:, jnp.asarray(tabs["b_cmp"])], 0.0)
    lo = ncp // 2 + CMP_NEAR[0]
    cmp_tab = jnp.pad(win, ((0, 0), (0, 0), (lo, ncp - lo - win.shape[-1])))
    return (toeplitz(tabs["b_near"], tabs["ok_near"], NEAR),
            toeplitz(tabs["b_win"], tabs["ok_win"], Q_BLOCK + WINDOW),
            cmp_tab.reshape(n_groups, NSA_GQA * Q_BLOCK, ncp))


def _key_blocks_t(a, g, front_blocks):
    b, s, _ = a.shape
    a = a.reshape(b, s // Q_BLOCK, Q_BLOCK, g, HEAD_DIM).transpose(0, 3, 1, 4, 2)
    return jnp.pad(a, ((0, 0), (0, 0), (front_blocks, 0), (0, 0), (0, 0)))


def _mixer(h, w_in, f_bias, cmp_pe, cmp_w1, cmp_w2, biases, tabs, b, s):
    d = h.shape[1]
    n_heads = d // HEAD_DIM
    hn = n_heads // 2
    hf = n_heads - hn
    g = hn // NSA_GQA
    gw = g * HEAD_DIM
    qn_w, fox_w = hn * HEAD_DIM, hf * HEAD_DIM
    splits = (qn_w, gw, gw, gw, gw, gw, gw, hn * 3, fox_w, fox_w, fox_w, hf)
    offs = np.concatenate([[0], np.cumsum(splits)])
    col = lambda k: w_in[:, offs[k]:offs[k + 1]]
    scale = HEAD_DIM ** -0.5 * LOG2E

    w_main = jnp.concatenate([col(k) for k in (0, 1, 2, 3, 4, 5, 6, 8, 9, 10)], axis=1).astype(BF16)
    blocks = np.cumsum([0, hn, g, g, g, g, g, g, hf, hf, hf])
    cscale = np.ones((1, w_main.shape[1]), np.float32)
    cscale[0, :qn_w] = scale
    cscale[0, blocks[7] * HEAD_DIM:blocks[8] * HEAD_DIM] = scale
    proj = _matmul(h, w_main, cscale=jnp.asarray(cscale), out_dtype=BF16)
    proj3 = proj.reshape(b, s, -1)
    grp = lambda k: proj3[:, :, blocks[k] * HEAD_DIM:blocks[k + 1] * HEAD_DIM]

    w_gate = jnp.pad(col(7).reshape(d, g, NSA_GQA * 3), ((0, 0), (0, 0), (0, LANES - NSA_GQA * 3))).reshape(d, g * LANES)
    w_f = jnp.pad(col(11), ((0, 0), (0, LANES - hf)))
    small = _matmul(h, jnp.concatenate([w_gate, w_f], axis=1).astype(BF16), tn=(g + 1) * LANES)
    small3 = small.reshape(b, s, -1)
    gates = small3[:, :, :g * LANES]
    f_raw = small3[:, :, g * LANES:]

    cum = _forget_cumsum(f_raw, jnp.pad(f_bias.astype(F32), (0, LANES - hf)).reshape(1, LANES))
    o_fox = _fox_attention(proj3, int(blocks[7]), int(blocks[8]), grp(9).reshape(b, s, hf, HEAD_DIM),
                           cum[:, :, :hf])

    ncp = tabs["ncp"]

    def to_rows(a):
        a = a.reshape(b, s // CMP_STRIDE, CMP_STRIDE, g, HEAD_DIM).transpose(0, 3, 1, 2, 4)
        a = a.reshape(b, g, s // CMP_STRIDE, CMP_STRIDE * HEAD_DIM)
        return jnp.pad(a, ((0, 0), (0, 0), (0, ncp - s // CMP_STRIDE), (0, 0)))

    rows = jnp.stack([to_rows(grp(1)), to_rows(grp(2))])
    half = CMP_STRIDE * HEAD_DIM
    w1cat = jnp.concatenate([cmp_w1[:, :half], cmp_w1[:, half:]], axis=2).astype(BF16)
    pe_flat = jnp.pad(cmp_pe.reshape(2, 1, CMP_BLOCK * HEAD_DIM), ((0, 0), (0, 7), (0, 0))).astype(BF16)
    kv_cmp = _compress(rows, w1cat, pe_flat, cmp_w2.astype(BF16))

    bias_near, bias_win, bias_cmp = biases
    o_cmp, negsel = _cmp_attention(proj3, kv_cmp, bias_cmp, jnp.asarray(tabs["overlap"], BF16), g)

    onehot = jnp.broadcast_to(jnp.asarray(tabs["onehot"], BF16)[None, :, None, :], (b, s, g, LANES))
    k_aug = jnp.concatenate([grp(3).reshape(b, s, g, HEAD_DIM), onehot], axis=-1).reshape(b, s, g * 2 * HEAD_DIM)
    front = lambda a, n: jnp.pad(a, ((0, 0), (n, 0), (0, 0)))
    o_nsa = _sel_win_attention(proj3, front(k_aug, Q_BLOCK), _key_blocks_t(grp(4), g, 1),
                               front(grp(5), WINDOW), _key_blocks_t(grp(6), g, WINDOW // Q_BLOCK),
                               negsel, o_cmp, gates, bias_near, bias_win, g)
    return o_nsa.reshape(b * s, qn_w), o_fox.reshape(b * s, fox_w)


def _pad_to(a, axis, mult):
    n = a.shape[axis]
    target = -(-n // mult) * mult
    if target == n:
        return a
    pad = [(0, 0)] * a.ndim
    pad[axis] = (0, target - n)
    return jnp.pad(a, pad)


def _dense_ffn(x, h, wg, wu, wd):
    ff_tile = 512
    wg = _pad_to(wg, 1, ff_tile).astype(BF16)
    wu = _pad_to(wu, 1, ff_tile).astype(BF16)
    wd = _pad_to(wd, 0, ff_tile).astype(BF16)
    act = _matmul(h, wg, w2=wu, out_dtype=BF16, tn=ff_tile)
    return _matmul(act, wd, res=x, tm=1024, tn=1024, tk=wd.shape[0] // 4)


def kernel(x, attn_norm, w_in, fgate_bias, cmp_pe, cmp_w1, cmp_w2, rel_bias, out_norm_nsa, out_norm_fox,
           w_out, ffn_norm, dense_w_gate, dense_w_up, dense_w_down, router_w, moe_w_gate, moe_w_up,
           moe_w_down, final_norm):
    b, s, d = x.shape
    depth = attn_norm.shape[0]
    n_groups = (d // HEAD_DIM // 2) // NSA_GQA
    assert s % SEL_TK == 0 and s // SEL_BLOCK <= LANES and d % (2 * NSA_GQA * HEAD_DIM) == 0
    assert (b * s) % MOE_TM == 0
    tabs = _static_tables(s)
    biases = _bias_tables(rel_bias, tabs, n_groups)
    xt = x.reshape(b * s, d)
    for layer in range(depth):
        h = _rmsnorm(xt, attn_norm[layer], BF16)
        o_nsa, o_fox = _mixer(h, w_in[layer], fgate_bias[layer], cmp_pe[layer], cmp_w1[layer], cmp_w2[layer],
                              biases, tabs, b, s)
        mixed = _pair_rmsnorm(o_nsa, o_fox, out_norm_nsa[layer], out_norm_fox[layer])
        xt = _matmul(mixed, w_out[layer].astype(BF16), res=xt)
        h = _rmsnorm(xt, ffn_norm[layer], BF16)
        i = layer // 2
        if layer % 2 == 0:
            xt = _dense_ffn(xt, h, dense_w_gate[i], dense_w_up[i], dense_w_down[i])
        else:
            xt = _moe_ffn(xt, h, router_w[i], moe_w_gate[i], moe_w_up[i], moe_w_down[i])
    return _rmsnorm(xt, final_norm, x.dtype).reshape(b, s, d)
```

```python
import functools
import math

import numpy as np
import jax
import jax.numpy as jnp
from jax import lax
from jax.experimental import pallas as pl
from jax.experimental.pallas import tpu as pltpu

F32 = jnp.float32
BF16 = jnp.bfloat16

HEAD_DIM = 128
NSA_GQA = 4
CMP_BLOCK = 32
CMP_STRIDE = 16
SEL_BLOCK = 64
SEL_TOPK = 16
WINDOW = 512
Q_BLOCK = 128
N_BUCKETS = 32
MAX_DISTANCE = 128
TOP_K = 2
EPS = 1e-6
NEG_INF = -1e30
FORCE_BONUS = 1e4

LANES = 128
VMEM_LIMIT = 56 * 1024 * 1024

LOG2E = 1.4426950408889634
MASK_BIG = 2.0 ** 101
REMOVED = -3.0e38
FAR_BUCKET = N_BUCKETS - 1
FAR_DIST = 113
SEL_TK = 512
NEAR = 2 * Q_BLOCK
CMP_NEAR = (-9, 7)
ONES_ROWS = 16
MOE_TM = 512
GATHER_ROWS = 512

_NT = (((1,), (1,)), ((), ()))


def _cparams(sem):
    return pltpu.CompilerParams(dimension_semantics=sem, vmem_limit_bytes=VMEM_LIMIT)


def _bucket_np(d):
    max_exact = N_BUCKETS // 2
    dd = np.maximum(d, 0)
    ratio = np.log(np.maximum(dd, max_exact).astype(np.float32) / max_exact) / math.log(MAX_DISTANCE / max_exact)
    large = np.minimum(max_exact + (ratio * (N_BUCKETS - max_exact)).astype(np.int32), N_BUCKETS - 1)
    return np.where(dd < max_exact, dd, large).astype(np.int32)


def _pick(n, prefs):
    for p in prefs:
        if n % p == 0:
            return p
    return n


def _rmsnorm_kernel(x_ref, g_ref, o_ref):
    x = x_ref[...].astype(F32)
    y = x * lax.rsqrt(jnp.mean(x * x, axis=-1, keepdims=True) + EPS)
    o_ref[...] = (y * g_ref[...]).astype(o_ref.dtype)


def _rmsnorm(x, gain, out_dtype):
    t, d = x.shape
    tm = _pick(t, (512, 256, 128))
    return pl.pallas_call(
        _rmsnorm_kernel,
        grid=(t // tm,),
        in_specs=[pl.BlockSpec((tm, d), lambda i: (i, 0)), pl.BlockSpec((1, d), lambda i: (0, 0))],
        out_specs=pl.BlockSpec((tm, d), lambda i: (i, 0)),
        out_shape=jax.ShapeDtypeStruct((t, d), out_dtype),
        compiler_params=_cparams(("parallel",)),
    )(x, gain.reshape(1, d).astype(F32))


def _pair_rmsnorm_kernel(a_ref, b_ref, ga_ref, gb_ref, o_ref):
    wa = a_ref.shape[-1]
    for ref, g, lo in ((a_ref, ga_ref, 0), (b_ref, gb_ref, wa)):
        x = ref[...].astype(F32)
        y = x * lax.rsqrt(jnp.mean(x * x, axis=-1, keepdims=True) + EPS)
        o_ref[:, lo:lo + x.shape[-1]] = (y * g[...]).astype(o_ref.dtype)


def _pair_rmsnorm(a, b, ga, gb):
    t, wa = a.shape
    wb = b.shape[1]
    tm = _pick(t, (512, 256, 128))
    return pl.pallas_call(
        _pair_rmsnorm_kernel,
        grid=(t // tm,),
        in_specs=[pl.BlockSpec((tm, wa), lambda i: (i, 0)), pl.BlockSpec((tm, wb), lambda i: (i, 0)),
                  pl.BlockSpec((1, wa), lambda i: (0, 0)), pl.BlockSpec((1, wb), lambda i: (0, 0))],
        out_specs=pl.BlockSpec((tm, wa + wb), lambda i: (i, 0)),
        out_shape=jax.ShapeDtypeStruct((t, wa + wb), BF16),
        compiler_params=_cparams(("parallel",)),
    )(a, b, ga.reshape(1, wa).astype(F32), gb.reshape(1, wb).astype(F32))


def _mm_kernel(*refs, nk, swiglu, has_cscale, has_res):
    it = iter(refs)
    x_ref = next(it)
    w_ref = next(it)
    w2_ref = next(it) if swiglu else None
    cs_ref = next(it) if has_cscale else None
    res_ref = next(it) if has_res else None
    o_ref = next(it)
    acc_ref = next(it) if nk > 1 else None
    acc2_ref = next(it) if (nk > 1 and swiglu) else None

    x = x_ref[...]

    def epilogue(r, r2):
        if swiglu:
            r = (r * jax.nn.sigmoid(r)) * r2
        if has_cscale:
            r = r * cs_ref[...]
        if has_res:
            r = res_ref[...] + r
        o_ref[...] = r.astype(o_ref.dtype)

    if nk == 1:
        r = jnp.dot(x, w_ref[...], preferred_element_type=F32)
        r2 = jnp.dot(x, w2_ref[...], preferred_element_type=F32) if swiglu else None
        epilogue(r, r2)
        return

    k = pl.program_id(2)

    def accum(acc, w):
        part = jnp.dot(x, w[...], preferred_element_type=F32)

        @pl.when(k == 0)
        def _():
            acc[...] = part

        @pl.when(k > 0)
        def _():
            acc[...] += part

    accum(acc_ref, w_ref)
    if swiglu:
        accum(acc2_ref, w2_ref)

    @pl.when(k == nk - 1)
    def _():
        epilogue(acc_ref[...], acc2_ref[...] if swiglu else None)


def _matmul(x, w, *, w2=None, cscale=None, res=None, out_dtype=F32, tm=1024, tn=512, tk=None):
    m, kdim = x.shape
    n = w.shape[1]
    tm = _pick(m, (tm, 512, 256, 128))
    tn = _pick(n, (tn, 512, 256, 128))
    tk = kdim if tk is None else tk
    assert kdim % tk == 0
    nk = kdim // tk
    swiglu = w2 is not None
    in_specs = [pl.BlockSpec((tm, tk), lambda i, j, k: (i, k)),
                pl.BlockSpec((tk, tn), lambda i, j, k: (k, j))]
    args = [x, w]
    if swiglu:
        in_specs.append(pl.BlockSpec((tk, tn), lambda i, j, k: (k, j)))
        args.append(w2)
    if cscale is not None:
        in_specs.append(pl.BlockSpec((1, tn), lambda i, j, k: (0, j)))
        args.append(cscale)
    if res is not None:
        in_specs.append(pl.BlockSpec((tm, tn), lambda i, j, k: (i, j)))
        args.append(res)
    scratch = []
    if nk > 1:
        scratch = [pltpu.VMEM((tm, tn), F32)] * (2 if swiglu else 1)
    kern = functools.partial(_mm_kernel, nk=nk, swiglu=swiglu, has_cscale=cscale is not None,
                             has_res=res is not None)
    return pl.pallas_call(
        kern,
        grid=(m // tm, n // tn, nk),
        in_specs=in_specs,
        out_specs=pl.BlockSpec((tm, tn), lambda i, j, k: (i, j)),
        out_shape=jax.ShapeDtypeStruct((m, n), out_dtype),
        scratch_shapes=scratch,
        compiler_params=_cparams(("parallel", "parallel", "arbitrary")),
    )(*args)


def _cumsum_kernel(f_ref, b_ref, o_ref, carry_ref):
    j = pl.program_id(1)

    @pl.when(j == 0)
    def _():
        carry_ref[...] = jnp.zeros_like(carry_ref)

    z = f_ref[...] + b_ref[...]
    logf = jnp.minimum(z, 0.0) - jnp.log1p(jnp.exp(-jnp.abs(z)))
    ts = z.shape[0]
    row = lax.broadcasted_iota(jnp.int32, (ts, ts), 0)
    col = lax.broadcasted_iota(jnp.int32, (ts, ts), 1)
    tri = jnp.where(col <= row, 1.0, 0.0).astype(F32)
    cum = jnp.dot(tri, logf, preferred_element_type=F32, precision=lax.Precision.HIGHEST)
    cum = cum + carry_ref[0:1, :]
    carry_ref[...] = jnp.broadcast_to(cum[ts - 1:ts, :], carry_ref.shape)
    o_ref[...] = cum


def _forget_cumsum(f_raw, f_bias):
    b, s, _ = f_raw.shape
    ts = _pick(s, (256, 128))
    return pl.pallas_call(
        _cumsum_kernel,
        grid=(b, s // ts),
        in_specs=[pl.BlockSpec((None, ts, LANES), lambda bi, j: (bi, j, 0)),
                  pl.BlockSpec((1, LANES), lambda bi, j: (0, 0))],
        out_specs=pl.BlockSpec((None, ts, LANES), lambda bi, j: (bi, j, 0)),
        out_shape=jax.ShapeDtypeStruct((b, s, LANES), F32),
        scratch_shapes=[pltpu.VMEM((8, LANES), F32)],
        compiler_params=_cparams(("parallel", "arbitrary")),
    )(f_raw, f_bias)


def _online_step(st, vt, m_ref, acc_ref):
    _online_steps([(st, vt)], m_ref, acc_ref)


def _online_steps(tiles, m_ref, acc_ref):
    m = m_ref[...]
    acc = acc_ref[...]
    for st, vt in tiles:
        m_new = jnp.maximum(m, jnp.max(st, axis=0, keepdims=True))
        alpha = jnp.exp2(m - m_new)
        p = jnp.exp2(st - m_new).astype(BF16)
        acc = alpha * acc + jnp.dot(vt, p, preferred_element_type=F32)
        m = m_new
    m_ref[...] = m
    acc_ref[...] = acc


def _softmax_init(m_ref, acc_ref):
    m_ref[...] = jnp.full_like(m_ref, NEG_INF)
    acc_ref[...] = jnp.zeros_like(acc_ref)


def _normalized(acc):
    return acc[:HEAD_DIM] / acc[HEAD_DIM:HEAD_DIM + 1]


def _with_ones_rows(vt):
    ones = jnp.ones(vt.shape[:-2] + (ONES_ROWS, vt.shape[-1]), vt.dtype)
    return jnp.concatenate([vt, ones], axis=-2)


def _lane_concat(blocks, n):
    return blocks[0] if n == 1 else jnp.concatenate([blocks[j] for j in range(n)], axis=1)


def _sweep_tiles(n_tiles, step):
    def pair(j, carry):
        step(2 * j, 2)
        return carry

    lax.fori_loop(0, n_tiles // 2, pair, 0)

    @pl.when(n_tiles % 2 == 1)
    def _():
        step(n_tiles - 1, 1)


def _fox_kernel(q_ref, k_ref, vt_ref, c_ref, o_ref, m_ref, acc_ref, ck_ref, *, tq):
    qi = pl.program_id(2)
    nq = c_ref.shape[0]

    @pl.when(qi == 0)
    def _():
        for j in range(nq):
            ck_ref[j * tq:(j + 1) * tq, :] = jnp.broadcast_to(c_ref[j:j + 1, :], (LANES, tq)).T

    q = q_ref[...]
    cq = c_ref[pl.ds(qi, 1), :]
    _softmax_init(m_ref, acc_ref)

    def step(ki, count, diagonal=False):
        tiles = []
        for sub in range(count):
            start = pl.multiple_of((ki + sub) * tq, tq)
            st = lax.dot_general(k_ref[pl.ds(start, tq), :], q, _NT, preferred_element_type=F32)
            ck = ck_ref[pl.ds(start, tq), :]
            st = st + cq - jnp.concatenate([ck] * (tq // LANES), axis=1)
            if diagonal:
                key = lax.broadcasted_iota(jnp.int32, st.shape, 0)
                qry = lax.broadcasted_iota(jnp.int32, st.shape, 1)
                st = jnp.where(key <= qry, st, -jnp.inf)
            tiles.append((st, vt_ref[ki + sub]))
        _online_steps(tiles, m_ref, acc_ref)

    _sweep_tiles(qi, step)
    step(qi, 1, diagonal=True)
    o_ref[...] = _normalized(acc_ref[...]).T


def _fox_attention(proj3, q0, k0, v, cum):
    b, s, h, hd = v.shape
    tq = _pick(s, (512, 256, 128))
    nq = s // tq
    c2 = (cum * LOG2E).transpose(0, 2, 1).reshape(b, h, nq, tq)
    vt = _with_ones_rows(v.reshape(b, nq, tq, h, hd).transpose(0, 3, 1, 4, 2))
    hv = hd + ONES_ROWS
    return pl.pallas_call(
        functools.partial(_fox_kernel, tq=tq),
        grid=(b, h, nq),
        in_specs=[pl.BlockSpec((None, tq, hd), lambda bi, hi, qi: (bi, qi, q0 + hi)),
                  pl.BlockSpec((None, s, hd), lambda bi, hi, qi: (bi, 0, k0 + hi)),
                  pl.BlockSpec((None, None, nq, hv, tq), lambda bi, hi, qi: (bi, hi, 0, 0, 0)),
                  pl.BlockSpec((None, None, nq, tq), lambda bi, hi, qi: (bi, hi, 0, 0))],
        out_specs=pl.BlockSpec((None, tq, hd), lambda bi, hi, qi: (bi, qi, hi)),
        out_shape=jax.ShapeDtypeStruct((b, s, h * hd), F32),
        scratch_shapes=[pltpu.VMEM((1, tq), F32), pltpu.VMEM((hv, tq), F32), pltpu.VMEM((s, LANES), F32)],
        compiler_params=_cparams(("parallel", "parallel", "arbitrary")),
    )(proj3, proj3, vt, c2)


def _compress_kernel(r_ref, w1_ref, pe_ref, w2_ref, o_ref):
    ab = jnp.dot(r_ref[...], w1_ref[...], preferred_element_type=F32)
    nc = ab.shape[0]
    half = pe_ref.shape[1] // 2
    first = ab[:, :HEAD_DIM]
    second = pltpu.roll(ab[:, HEAD_DIM:], nc - 1, axis=0)
    pe_term = (jnp.dot(pe_ref[:, :half], w1_ref[:, :HEAD_DIM], preferred_element_type=F32)
               + jnp.dot(pe_ref[:, half:], w1_ref[:, HEAD_DIM:], preferred_element_type=F32))
    pre = first + second + pe_term[0:1, :]
    hid = pre * jax.nn.sigmoid(pre)
    o_ref[...] = jnp.dot(hid.astype(BF16), w2_ref[...], preferred_element_type=F32).astype(o_ref.dtype)


def _compress(rows, w1cat, pe_flat, w2):
    two, b, g, nc, width = rows.shape
    return pl.pallas_call(
        _compress_kernel,
        grid=(two, b, g),
        in_specs=[pl.BlockSpec((None, None, None, nc, width), lambda a, bi, gi: (a, bi, gi, 0, 0)),
                  pl.BlockSpec((None, width, 2 * HEAD_DIM), lambda a, bi, gi: (a, 0, 0)),
                  pl.BlockSpec((None, 8, 2 * width), lambda a, bi, gi: (a, 0, 0)),
                  pl.BlockSpec((None, HEAD_DIM, HEAD_DIM), lambda a, bi, gi: (a, 0, 0))],
        out_specs=pl.BlockSpec((None, None, None, nc, HEAD_DIM), lambda a, bi, gi: (a, bi, gi, 0, 0)),
        out_shape=jax.ShapeDtypeStruct((two, b, g, nc, HEAD_DIM), BF16),
        compiler_params=_cparams(("parallel", "parallel", "parallel")),
    )(rows, w1cat, pe_flat, w2)


def _stack_heads(q_ref):
    return jnp.concatenate([q_ref[:, r * HEAD_DIM:(r + 1) * HEAD_DIM] for r in range(NSA_GQA)], axis=0)


def _cmp_attn_kernel(q_ref, kc_ref, vc_ref, gd_ref, ov_ref, oc_ref, ns_ref, *, n_sel, top_n):
    i = pl.program_id(2)
    t0 = i * Q_BLOCK
    q4 = _stack_heads(q_ref)
    s = lax.dot_general(q4, kc_ref[...], _NT, preferred_element_type=F32)
    ncp = s.shape[1]
    shift = lax.rem(i * (Q_BLOCK // CMP_STRIDE) + ncp // 2, ncp)
    bias = pltpu.roll(gd_ref[...], shift, axis=1)
    t = t0 + (lax.broadcasted_iota(jnp.int32, s.shape, 0) & (Q_BLOCK - 1))
    cmp_end = lax.broadcasted_iota(jnp.int32, s.shape, 1) * CMP_STRIDE + (CMP_BLOCK - 1)
    s = jnp.where(t >= cmp_end, s + bias, -jnp.inf)
    m = jnp.maximum(jnp.max(s, axis=1, keepdims=True), NEG_INF)
    p = jnp.exp2(s - m)
    l = jnp.sum(p, axis=1, keepdims=True)
    p = (p * jnp.where(l > 0.0, 1.0 / l, 0.0)).astype(BF16)
    o = jnp.dot(p, vc_ref[...], preferred_element_type=F32)
    for r in range(NSA_GQA):
        oc_ref[:, r * HEAD_DIM:(r + 1) * HEAD_DIM] = o[r * Q_BLOCK:(r + 1) * Q_BLOCK]
    imp4 = jnp.dot(p, ov_ref[...], preferred_element_type=F32)
    imp = imp4[0:Q_BLOCK]
    for r in range(1, NSA_GQA):
        imp = imp + imp4[r * Q_BLOCK:(r + 1) * Q_BLOCK]

    shape = (LANES, Q_BLOCK)
    blk = lax.broadcasted_iota(jnp.int32, shape, 0).astype(F32)
    cur = ((t0 + lax.broadcasted_iota(jnp.int32, shape, 1)) // SEL_BLOCK).astype(F32)
    visible = blk <= cur
    forced = (blk == 0.0) | (blk == cur) | (blk == cur - 1.0)
    score = jnp.where(visible, imp.T + jnp.where(forced, FORCE_BONUS, 0.0), NEG_INF)
    score = jnp.where(blk < float(n_sel), score, REMOVED)

    def pick(_, carry):
        score, sel = carry
        mx = jnp.max(score, axis=0, keepdims=True)
        first = jnp.min(jnp.where(score == mx, blk, float(LANES)), axis=0, keepdims=True)
        hit = blk == first
        return jnp.where(hit, REMOVED, score), jnp.where(hit, 1.0, sel)

    _, sel = lax.fori_loop(0, top_n, pick, (score, jnp.zeros(shape, F32)))
    ns_ref[...] = jnp.where((sel > 0.0) & visible, 0.0, -MASK_BIG).T.astype(ns_ref.dtype)


def _cmp_attention(proj3, kv_cmp, gd, overlap, n_groups):
    b, s, _ = proj3.shape
    ncp = kv_cmp.shape[3]
    n_sel = s // SEL_BLOCK
    width = NSA_GQA * HEAD_DIM
    kern = functools.partial(_cmp_attn_kernel, n_sel=n_sel, top_n=min(SEL_TOPK, n_sel))
    return pl.pallas_call(
        kern,
        grid=(b, n_groups, s // Q_BLOCK),
        in_specs=[pl.BlockSpec((None, Q_BLOCK, width), lambda bi, g, i: (bi, i, g)),
                  pl.BlockSpec((None, None, None, ncp, HEAD_DIM), lambda bi, g, i: (0, bi, g, 0, 0)),
                  pl.BlockSpec((None, None, None, ncp, HEAD_DIM), lambda bi, g, i: (1, bi, g, 0, 0)),
                  pl.BlockSpec((None, NSA_GQA * Q_BLOCK, ncp), lambda bi, g, i: (g, 0, 0)),
                  pl.BlockSpec((ncp, LANES), lambda bi, g, i: (0, 0))],
        out_specs=[pl.BlockSpec((None, Q_BLOCK, width), lambda bi, g, i: (bi, i, g)),
                   pl.BlockSpec((None, None, Q_BLOCK, LANES), lambda bi, g, i: (bi, g, i, 0))],
        out_shape=[jax.ShapeDtypeStruct((b, s, n_groups * width), F32),
                   jax.ShapeDtypeStruct((b, n_groups, s, LANES), BF16)],
        compiler_params=_cparams(("parallel", "parallel", "parallel")),
    )(proj3, kv_cmp, kv_cmp, gd, overlap)


def _sel_win_kernel(q_ref, ka_ref, vst_ref, kw_ref, vwt_ref, ns_ref, oc_ref, g_ref, bn_ref, bw_ref,
                    o_ref, m_ref, acc_ref):
    i = pl.program_id(2)
    t0 = pl.multiple_of(i * Q_BLOCK, Q_BLOCK)
    q4 = _stack_heads(q_ref)

    ns = ns_ref[...]
    lane = lax.broadcasted_iota(jnp.int32, ns.shape, 1)
    near_blk = (lane >= 2 * i - 2) & (lane <= 2 * i + 1)
    ns_far = jnp.where(near_blk, -MASK_BIG, ns.astype(F32)).astype(BF16)
    qa_far = jnp.concatenate([q4, jnp.concatenate([ns_far] * NSA_GQA, axis=0)], axis=1)
    qa_near = jnp.concatenate([q4, jnp.concatenate([ns] * NSA_GQA, axis=0)], axis=1)
    _softmax_init(m_ref, acc_ref)

    blocks_per_tile = SEL_TK // Q_BLOCK

    def far_step(j, count):
        tiles = []
        for sub in range(count):
            start = pl.multiple_of(Q_BLOCK + (j + sub) * SEL_TK, Q_BLOCK)
            st = lax.dot_general(ka_ref[pl.ds(start, SEL_TK), :], qa_far, _NT, preferred_element_type=F32)
            vt = _lane_concat(vst_ref[pl.ds(1 + (j + sub) * blocks_per_tile, blocks_per_tile)], blocks_per_tile)
            tiles.append((st, vt))
        _online_steps(tiles, m_ref, acc_ref)

    _sweep_tiles(jnp.where(i == 0, 0, (i + 2) // 4), far_step)

    st = lax.dot_general(ka_ref[pl.ds(t0, NEAR), :], qa_near, _NT, preferred_element_type=F32)
    st = st + bn_ref[...]
    key_pos = t0 - Q_BLOCK + lax.broadcasted_iota(jnp.int32, st.shape, 0)
    st = jnp.where(key_pos >= 0, st, -jnp.inf)
    _online_step(st, _lane_concat(vst_ref[pl.ds(i, 2)], 2), m_ref, acc_ref)
    o_sel = _normalized(acc_ref[...])

    span = Q_BLOCK + WINDOW
    sw = lax.dot_general(kw_ref[pl.ds(t0, span), :], q4, _NT, preferred_element_type=F32)
    sw = sw + bw_ref[...]
    key_pos = t0 - WINDOW + lax.broadcasted_iota(jnp.int32, sw.shape, 0)
    sw = jnp.where(key_pos >= 0, sw, -jnp.inf)
    pw = jnp.exp2(sw - jnp.max(sw, axis=0, keepdims=True)).astype(BF16)
    n_blk = span // Q_BLOCK
    o_win = _normalized(jnp.dot(_lane_concat(vwt_ref[pl.ds(i, n_blk)], n_blk), pw,
                                preferred_element_type=F32))

    gate = jax.nn.sigmoid(g_ref[...])
    gate_t = gate.T
    for r in range(NSA_GQA):
        lo, hi = r * Q_BLOCK, (r + 1) * Q_BLOCK
        mix_t = gate_t[3 * r + 1:3 * r + 2, :] * o_sel[:, lo:hi] + gate_t[3 * r + 2:3 * r + 3, :] * o_win[:, lo:hi]
        o_ref[:, lo:hi] = gate[:, 3 * r:3 * r + 1] * oc_ref[:, lo:hi] + mix_t.T


def _sel_win_attention(proj3, k_aug, vt_sel, k_win, vt_win, negsel, o_cmp, gates, bias_near, bias_win, n_groups):
    b, s, _ = proj3.shape
    width = NSA_GQA * HEAD_DIM
    rows = NSA_GQA * Q_BLOCK
    sp_sel = k_aug.shape[1]
    sp_win = k_win.shape[1]
    nb_sel = vt_sel.shape[2]
    nb_win = vt_win.shape[2]
    return pl.pallas_call(
        _sel_win_kernel,
        grid=(b, n_groups, s // Q_BLOCK),
        in_specs=[pl.BlockSpec((None, Q_BLOCK, width), lambda bi, g, i: (bi, i, g)),
                  pl.BlockSpec((None, sp_sel, 2 * HEAD_DIM), lambda bi, g, i: (bi, 0, g)),
                  pl.BlockSpec((None, None, nb_sel, HEAD_DIM + ONES_ROWS, Q_BLOCK), lambda bi, g, i: (bi, g, 0, 0, 0)),
                  pl.BlockSpec((None, sp_win, HEAD_DIM), lambda bi, g, i: (bi, 0, g)),
                  pl.BlockSpec((None, None, nb_win, HEAD_DIM + ONES_ROWS, Q_BLOCK), lambda bi, g, i: (bi, g, 0, 0, 0)),
                  pl.BlockSpec((None, None, Q_BLOCK, LANES), lambda bi, g, i: (bi, g, i, 0)),
                  pl.BlockSpec((None, Q_BLOCK, width), lambda bi, g, i: (bi, i, g)),
                  pl.BlockSpec((None, Q_BLOCK, LANES), lambda bi, g, i: (bi, i, g)),
                  pl.BlockSpec((None, NEAR, rows), lambda bi, g, i: (g, 0, 0)),
                  pl.BlockSpec((None, Q_BLOCK + WINDOW, rows), lambda bi, g, i: (g, 0, 0))],
        out_specs=pl.BlockSpec((None, Q_BLOCK, width), lambda bi, g, i: (bi, i, g)),
        out_shape=jax.ShapeDtypeStruct((b, s, n_groups * width), F32),
        scratch_shapes=[pltpu.VMEM((1, rows), F32), pltpu.VMEM((HEAD_DIM + ONES_ROWS, rows), F32)],
        compiler_params=_cparams(("parallel", "parallel", "arbitrary")),
    )(proj3, k_aug, vt_sel, k_win, vt_win, negsel, o_cmp, gates, bias_near, bias_win)


def _router_kernel(h_ref, w_ref, o_ref, *, n_experts):
    logits = jnp.dot(h_ref[...], w_ref[...], preferred_element_type=F32)
    lane = lax.broadcasted_iota(jnp.int32, logits.shape, 1).astype(F32)
    logits = jnp.where(lane < float(n_experts), logits, -jnp.inf)
    m1 = jnp.max(logits, axis=1, keepdims=True)
    i1 = jnp.min(jnp.where(logits == m1, lane, float(LANES)), axis=1, keepdims=True)
    rest = jnp.where(lane == i1, -jnp.inf, logits)
    m2 = jnp.max(rest, axis=1, keepdims=True)
    i2 = jnp.min(jnp.where(rest == m2, lane, float(LANES)), axis=1, keepdims=True)
    e2 = jnp.exp(m2 - m1)
    denom = 1.0 + e2
    o_ref[...] = (jnp.where(lane == 0.0, i1, 0.0) + jnp.where(lane == 1.0, i2, 0.0)
                  + jnp.where(lane == 2.0, 1.0 / denom, 0.0) + jnp.where(lane == 3.0, e2 / denom, 0.0))


def _router(h, w_router_padded, n_experts):
    t, d = h.shape
    tm = _pick(t, (512, 256, 128))
    return pl.pallas_call(
        functools.partial(_router_kernel, n_experts=n_experts),
        grid=(t // tm,),
        in_specs=[pl.BlockSpec((tm, d), lambda i: (i, 0)), pl.BlockSpec((d, LANES), lambda i: (0, 0))],
        out_specs=pl.BlockSpec((tm, LANES), lambda i: (i, 0)),
        out_shape=jax.ShapeDtypeStruct((t, LANES), F32),
        compiler_params=_cparams(("parallel",)),
    )(h, w_router_padded)


def _row_copy(src_ref, dst_ref, sem, src_row, dst_row):
    return pltpu.make_async_copy(src_ref.at[pl.ds(src_row, 1)], dst_ref.at[pl.ds(dst_row, 1)], sem)


def _gather_rows_kernel(idx_ref, src_ref, out_ref, sem, *, rows):
    base = pl.program_id(0) * rows

    def start(r2, carry):
        for lane in range(2):
            r = 2 * r2 + lane
            _row_copy(src_ref, out_ref, sem, idx_ref[base + r], r).start(priority=lane)
        return carry

    lax.fori_loop(0, rows // 2, start, 0)

    def wait(r, carry):
        _row_copy(src_ref, out_ref, sem, 0, r).wait()
        return carry

    lax.fori_loop(0, rows, wait, 0)


def _gather_rows(src, idx):
    n_out = idx.shape[0]
    n_src, width = src.shape
    rows = _pick(n_out, (GATHER_ROWS, 256, 128))
    src = src.reshape(n_src, width // LANES, LANES)
    out = pl.pallas_call(
        functools.partial(_gather_rows_kernel, rows=rows),
        grid_spec=pltpu.PrefetchScalarGridSpec(
            num_scalar_prefetch=1,
            grid=(n_out // rows,),
            in_specs=[pl.BlockSpec(memory_space=pl.ANY)],
            out_specs=pl.BlockSpec((rows,) + src.shape[1:], lambda c, idx_ref: (c, 0, 0)),
            scratch_shapes=[pltpu.SemaphoreType.DMA(())]),
        out_shape=jax.ShapeDtypeStruct((n_out,) + src.shape[1:], src.dtype),
        compiler_params=_cparams(("arbitrary",)),
    )(idx, src)
    return out.reshape(n_out, width)


def _gmm_kernel(te_ref, nv_ref, x_ref, w_ref, *rest, swiglu):
    o_ref = rest[-1]
    live = pl.program_id(0) < nv_ref[0]

    @pl.when(live)
    def _():
        x = x_ref[...]
        r = jnp.dot(x, w_ref[...], preferred_element_type=F32)
        if swiglu:
            r = (r * jax.nn.sigmoid(r)) * jnp.dot(x, rest[0][...], preferred_element_type=F32)
        o_ref[...] = r.astype(o_ref.dtype)

    @pl.when(jnp.logical_not(live))
    def _():
        o_ref[...] = jnp.zeros_like(o_ref)


def _grouped_matmul(x, w, tile_expert, n_valid, *, w2=None, tn=512):
    p, kdim = x.shape
    n = w.shape[2]
    tn = _pick(n, (tn, 256, 128))
    wspec = pl.BlockSpec((None, kdim, tn), lambda i, j, te, nv: (te[i], 0, j))
    in_specs = [pl.BlockSpec((MOE_TM, kdim), lambda i, j, te, nv: (i, 0)), wspec]
    args = [x, w]
    if w2 is not None:
        in_specs.append(wspec)
        args.append(w2)
    return pl.pallas_call(
        functools.partial(_gmm_kernel, swiglu=w2 is not None),
        grid_spec=pltpu.PrefetchScalarGridSpec(
            num_scalar_prefetch=2,
            grid=(p // MOE_TM, n // tn),
            in_specs=in_specs,
            out_specs=pl.BlockSpec((MOE_TM, tn), lambda i, j, te, nv: (i, j))),
        out_shape=jax.ShapeDtypeStruct((p, n), BF16),
        compiler_params=_cparams(("parallel", "parallel")),
    )(tile_expert, n_valid, *args)


def _combine_kernel(x_ref, ya_ref, yb_ref, r_ref, o_ref):
    route = r_ref[...]
    o_ref[...] = (x_ref[...] + route[:, 2:3] * ya_ref[...].astype(F32)
                  + route[:, 3:4] * yb_ref[...].astype(F32))


def _moe_combine(x, y2, route):
    t, d = x.shape
    tm = _pick(t, (256, 128))
    nt = t // tm
    return pl.pallas_call(
        _combine_kernel,
        grid=(nt,),
        in_specs=[pl.BlockSpec((tm, d), lambda i: (i, 0)),
                  pl.BlockSpec((tm, d), lambda i: (i, 0)),
                  pl.BlockSpec((tm, d), lambda i: (i + nt, 0)),
                  pl.BlockSpec((tm, LANES), lambda i: (i, 0))],
        out_specs=pl.BlockSpec((tm, d), lambda i: (i, 0)),
        out_shape=jax.ShapeDtypeStruct((t, d), F32),
        compiler_params=_cparams(("parallel",)),
    )(x, y2, y2, route)


def _route_plan(i1, i2, n_experts):
    t = i1.shape[0]
    e_flat = jnp.concatenate([i1, i2])
    onehot = (e_flat[:, None] == jnp.arange(n_experts, dtype=jnp.int32)[None, :]).astype(jnp.int32)
    csum = jnp.cumsum(onehot, axis=0)
    counts = csum[-1]
    rank = jnp.sum((csum - onehot) * onehot, axis=1)
    padded = ((counts + MOE_TM - 1) // MOE_TM) * MOE_TM
    ends = jnp.cumsum(padded)
    pos = jnp.sum(onehot * (ends - padded)[None, :], axis=1) + rank
    p_rows = 2 * t + n_experts * MOE_TM
    token = jnp.tile(jnp.arange(t, dtype=jnp.int32), 2)
    src = jnp.zeros((p_rows,), jnp.int32).at[pos].set(token)
    tile_start = jnp.arange(p_rows // MOE_TM, dtype=jnp.int32) * MOE_TM
    tile_expert = jnp.minimum(jnp.sum((tile_start[:, None] >= ends[None, :]).astype(jnp.int32), axis=1),
                              n_experts - 1)
    n_valid = (ends[-1] // MOE_TM).reshape(1)
    return pos.astype(jnp.int32), src, tile_expert.astype(jnp.int32), n_valid.astype(jnp.int32)


def _moe_ffn(x, h, router_w, wg, wu, wd):
    n_experts = router_w.shape[1]
    route = _router(h, jnp.pad(router_w, ((0, 0), (0, LANES - n_experts))).astype(BF16), n_experts)
    i1 = route[:, 0].astype(jnp.int32)
    i2 = route[:, 1].astype(jnp.int32)
    pos, src, tile_expert, n_valid = _route_plan(i1, i2, n_experts)
    xs = _gather_rows(h, src)
    act = _grouped_matmul(xs, wg.astype(BF16), tile_expert, n_valid, w2=wu.astype(BF16))
    y = _grouped_matmul(act, wd.astype(BF16), tile_expert, n_valid)
    return _moe_combine(x, _gather_rows(y, pos), route)


def _static_tables(s):
    n_sel = s // SEL_BLOCK
    ncp = max(LANES, -(-(s // CMP_STRIDE) // LANES) * LANES)
    n_cmp = (s - CMP_BLOCK) // CMP_STRIDE + 1
    span = Q_BLOCK + WINDOW
    d_near = (NEAR - 1) - np.arange(NEAR + Q_BLOCK - 1)
    d_win = (span - 1) - np.arange(span + Q_BLOCK - 1)
    r = np.arange(Q_BLOCK)[:, None]
    d_cmp = r - CMP_STRIDE * np.arange(CMP_NEAR[0], CMP_NEAR[1])[None, :] - (CMP_BLOCK - 1)
    c0 = np.arange(ncp)[:, None] * CMP_STRIDE
    j0 = np.arange(LANES)[None, :] * SEL_BLOCK
    overlap = np.clip(np.minimum(c0 + CMP_BLOCK, j0 + SEL_BLOCK) - np.maximum(c0, j0), 0, None).astype(np.float32) / CMP_BLOCK
    overlap[n_cmp:, :] = 0.0
    overlap[:, n_sel:] = 0.0
    onehot = (np.arange(s)[:, None] // SEL_BLOCK == np.arange(LANES)[None, :]).astype(np.float32)
    return dict(
        ncp=ncp,
        b_near=_bucket_np(d_near), ok_near=d_near >= 0,
        b_win=_bucket_np(d_win), ok_win=(d_win >= 0) & (d_win < WINDOW),
        b_cmp=_bucket_np(d_cmp), ok_cmp=(d_cmp >= 0) & (d_cmp < FAR_DIST),
        overlap=overlap, onehot=onehot)


def _bias_tables(rel_bias, tabs, n_groups):
    tbl = rel_bias.T.astype(F32) * LOG2E
    reb = tbl - tbl[:, FAR_BUCKET:FAR_BUCKET + 1]
    heads = reb.shape[0]

    def toeplitz(bucket, ok, width):
        seq = jnp.where(jnp.asarray(ok)[None, :], reb[:, jnp.asarray(bucket)], -MASK_BIG)
        tile = jnp.stack([seq[:, Q_BLOCK - 1 - r:Q_BLOCK - 1 - r + width] for r in range(Q_BLOCK)], axis=1)
        tile = tile.reshape(n_groups, NSA_GQA, Q_BLOCK, width).transpose(0, 3, 1, 2)
        return tile.reshape(n_groups, width, NSA_GQA * Q_BLOCK)

    ncp = tabs["ncp"]
    win = jnp.where(jnp.asarray(tabs["ok_cmp"])[None], reb[:, jnp.asarray(tabs["b_cmp"])], 0.0)
    lo = ncp // 2 + CMP_NEAR[0]
    cmp_tab = jnp.pad(win, ((0, 0), (0, 0), (lo, ncp - lo - win.shape[-1])))
    return (toeplitz(tabs["b_near"], tabs["ok_near"], NEAR),
            toeplitz(tabs["b_win"], tabs["ok_win"], Q_BLOCK + WINDOW),
            cmp_tab.reshape(n_groups, NSA_GQA * Q_BLOCK, ncp))


def _key_blocks_t(a, g, front_blocks):
    b, s, _ = a.shape
    a = _with_ones_rows(a.reshape(b, s // Q_BLOCK, Q_BLOCK, g, HEAD_DIM).transpose(0, 3, 1, 4, 2))
    return jnp.pad(a, ((0, 0), (0, 0), (front_blocks, 0), (0, 0), (0, 0)))


def _mixer(h, w_in, f_bias, cmp_pe, cmp_w1, cmp_w2, biases, tabs, b, s):
    d = h.shape[1]
    n_heads = d // HEAD_DIM
    hn = n_heads // 2
    hf = n_heads - hn
    g = hn // NSA_GQA
    gw = g * HEAD_DIM
    qn_w, fox_w = hn * HEAD_DIM, hf * HEAD_DIM
    splits = (qn_w, gw, gw, gw, gw, gw, gw, hn * 3, fox_w, fox_w, fox_w, hf)
    offs = np.concatenate([[0], np.cumsum(splits)])
    col = lambda k: w_in[:, offs[k]:offs[k + 1]]
    scale = HEAD_DIM ** -0.5 * LOG2E

    w_main = jnp.concatenate([col(k) for k in (0, 1, 2, 3, 4, 5, 6, 8, 9, 10)], axis=1).astype(BF16)
    blocks = np.cumsum([0, hn, g, g, g, g, g, g, hf, hf, hf])
    cscale = np.ones((1, w_main.shape[1]), np.float32)
    cscale[0, :qn_w] = scale
    cscale[0, blocks[7] * HEAD_DIM:blocks[8] * HEAD_DIM] = scale
    proj = _matmul(h, w_main, cscale=jnp.asarray(cscale), out_dtype=BF16)
    proj3 = proj.reshape(b, s, -1)
    grp = lambda k: proj3[:, :, blocks[k] * HEAD_DIM:blocks[k + 1] * HEAD_DIM]

    w_gate = jnp.pad(col(7).reshape(d, g, NSA_GQA * 3), ((0, 0), (0, 0), (0, LANES - NSA_GQA * 3))).reshape(d, g * LANES)
    w_f = jnp.pad(col(11), ((0, 0), (0, LANES - hf)))
    small = _matmul(h, jnp.concatenate([w_gate, w_f], axis=1).astype(BF16), tn=(g + 1) * LANES)
    small3 = small.reshape(b, s, -1)
    gates = small3[:, :, :g * LANES]
    f_raw = small3[:, :, g * LANES:]

    cum = _forget_cumsum(f_raw, jnp.pad(f_bias.astype(F32), (0, LANES - hf)).reshape(1, LANES))
    o_fox = _fox_attention(proj3, int(blocks[7]), int(blocks[8]), grp(9).reshape(b, s, hf, HEAD_DIM),
                           cum[:, :, :hf])

    ncp = tabs["ncp"]

    def to_rows(a):
        a = a.reshape(b, s // CMP_STRIDE, CMP_STRIDE, g, HEAD_DIM).transpose(0, 3, 1, 2, 4)
        a = a.reshape(b, g, s // CMP_STRIDE, CMP_STRIDE * HEAD_DIM)
        return jnp.pad(a, ((0, 0), (0, 0), (0, ncp - s // CMP_STRIDE), (0, 0)))

    rows = jnp.stack([to_rows(grp(1)), to_rows(grp(2))])
    half = CMP_STRIDE * HEAD_DIM
    w1cat = jnp.concatenate([cmp_w1[:, :half], cmp_w1[:, half:]], axis=2).astype(BF16)
    pe_flat = jnp.pad(cmp_pe.reshape(2, 1, CMP_BLOCK * HEAD_DIM), ((0, 0), (0, 7), (0, 0))).astype(BF16)
    kv_cmp = _compress(rows, w1cat, pe_flat, cmp_w2.astype(BF16))

    bias_near, bias_win, bias_cmp = biases
    o_cmp, negsel = _cmp_attention(proj3, kv_cmp, bias_cmp, jnp.asarray(tabs["overlap"], BF16), g)

    onehot = jnp.broadcast_to(jnp.asarray(tabs["onehot"], BF16)[None, :, None, :], (b, s, g, LANES))
    k_aug = jnp.concatenate([grp(3).reshape(b, s, g, HEAD_DIM), onehot], axis=-1).reshape(b, s, g * 2 * HEAD_DIM)
    front = lambda a, n: jnp.pad(a, ((0, 0), (n, 0), (0, 0)))
    o_nsa = _sel_win_attention(proj3, front(k_aug, Q_BLOCK), _key_blocks_t(grp(4), g, 1),
                               front(grp(5), WINDOW), _key_blocks_t(grp(6), g, WINDOW // Q_BLOCK),
                               negsel, o_cmp, gates, bias_near, bias_win, g)
    return o_nsa.reshape(b * s, qn_w), o_fox.reshape(b * s, fox_w)


def _pad_to(a, axis, mult):
    n = a.shape[axis]
    target = -(-n // mult) * mult
    if target == n:
        return a
    pad = [(0, 0)] * a.ndim
    pad[axis] = (0, target - n)
    return jnp.pad(a, pad)


def _dense_ffn(x, h, wg, wu, wd):
    ff_tile = 512
    wg = _pad_to(wg, 1, ff_tile).astype(BF16)
    wu = _pad_to(wu, 1, ff_tile).astype(BF16)
    wd = _pad_to(wd, 0, ff_tile).astype(BF16)
    act = _matmul(h, wg, w2=wu, out_dtype=BF16, tn=ff_tile)
    return _matmul(act, wd, res=x, tm=1024, tn=1024, tk=wd.shape[0] // 4)


def kernel(x, attn_norm, w_in, fgate_bias, cmp_pe, cmp_w1, cmp_w2, rel_bias, out_norm_nsa, out_norm_fox,
           w_out, ffn_norm, dense_w_gate, dense_w_up, dense_w_down, router_w, moe_w_gate, moe_w_up,
           moe_w_down, final_norm):
    b, s, d = x.shape
    depth = attn_norm.shape[0]
    n_groups = (d // HEAD_DIM // 2) // NSA_GQA
    assert s % SEL_TK == 0 and s // SEL_BLOCK <= LANES and d % (2 * NSA_GQA * HEAD_DIM) == 0
    assert (b * s) % MOE_TM == 0
    tabs = _static_tables(s)
    biases = _bias_tables(rel_bias, tabs, n_groups)
    xt = x.reshape(b * s, d)
    for layer in range(depth):
        h = _rmsnorm(xt, attn_norm[layer], BF16)
        o_nsa, o_fox = _mixer(h, w_in[layer], fgate_bias[layer], cmp_pe[layer], cmp_w1[layer], cmp_w2[layer],
                              biases, tabs, b, s)
        mixed = _pair_rmsnorm(o_nsa, o_fox, out_norm_nsa[layer], out_norm_fox[layer])
        xt = _matmul(mixed, w_out[layer].astype(BF16), res=xt)
        h = _rmsnorm(xt, ffn_norm[layer], BF16)
        i = layer // 2
        if layer % 2 == 0:
            xt = _dense_ffn(xt, h, dense_w_gate[i], dense_w_up[i], dense_w_down[i])
        else:
            xt = _moe_ffn(xt, h, router_w[i], moe_w_gate[i], moe_w_up[i], moe_w_down[i])
    return _rmsnorm(xt, final_norm, x.dtype).reshape(b, s, d)
```

```python
import functools
import math

import numpy as np
import jax
import jax.numpy as jnp
from jax import lax
from jax.experimental import pallas as pl
from jax.experimental.pallas import tpu as pltpu

F32 = jnp.float32
BF16 = jnp.bfloat16

HEAD_DIM = 128
NSA_GQA = 4
CMP_BLOCK = 32
CMP_STRIDE = 16
SEL_BLOCK = 64
SEL_TOPK = 16
WINDOW = 512
Q_BLOCK = 128
N_BUCKETS = 32
MAX_DISTANCE = 128
TOP_K = 2
EPS = 1e-6
NEG_INF = -1e30
FORCE_BONUS = 1e4

LANES = 128
VMEM_LIMIT = 56 * 1024 * 1024

LOG2E = 1.4426950408889634
MASK_BIG = 2.0 ** 101
REMOVED = -3.0e38
FAR_BUCKET = N_BUCKETS - 1
FAR_DIST = 113
SEL_TK = 512
NEAR = 2 * Q_BLOCK
CMP_NEAR = (-9, 7)
ONES_ROWS = 16
MOE_TM = 512
GATHER_ROWS = 512

_NT = (((1,), (1,)), ((), ()))


def _cparams(sem):
    return pltpu.CompilerParams(dimension_semantics=sem, vmem_limit_bytes=VMEM_LIMIT)


def _bucket_np(d):
    max_exact = N_BUCKETS // 2
    dd = np.maximum(d, 0)
    ratio = np.log(np.maximum(dd, max_exact).astype(np.float32) / max_exact) / math.log(MAX_DISTANCE / max_exact)
    large = np.minimum(max_exact + (ratio * (N_BUCKETS - max_exact)).astype(np.int32), N_BUCKETS - 1)
    return np.where(dd < max_exact, dd, large).astype(np.int32)


def _pick(n, prefs):
    for p in prefs:
        if n % p == 0:
            return p
    return n


def _rmsnorm_kernel(x_ref, g_ref, o_ref):
    x = x_ref[...].astype(F32)
    y = x * lax.rsqrt(jnp.mean(x * x, axis=-1, keepdims=True) + EPS)
    o_ref[...] = (y * g_ref[...]).astype(o_ref.dtype)


def _rmsnorm(x, gain, out_dtype):
    t, d = x.shape
    tm = _pick(t, (512, 256, 128))
    return pl.pallas_call(
        _rmsnorm_kernel,
        grid=(t // tm,),
        in_specs=[pl.BlockSpec((tm, d), lambda i: (i, 0)), pl.BlockSpec((1, d), lambda i: (0, 0))],
        out_specs=pl.BlockSpec((tm, d), lambda i: (i, 0)),
        out_shape=jax.ShapeDtypeStruct((t, d), out_dtype),
        compiler_params=_cparams(("parallel",)),
    )(x, gain.reshape(1, d).astype(F32))


def _pair_rmsnorm_kernel(a_ref, b_ref, ga_ref, gb_ref, o_ref):
    wa = a_ref.shape[-1]
    for ref, g, lo in ((a_ref, ga_ref, 0), (b_ref, gb_ref, wa)):
        x = ref[...].astype(F32)
        y = x * lax.rsqrt(jnp.mean(x * x, axis=-1, keepdims=True) + EPS)
        o_ref[:, lo:lo + x.shape[-1]] = (y * g[...]).astype(o_ref.dtype)


def _pair_rmsnorm(a, b, ga, gb):
    t, wa = a.shape
    wb = b.shape[1]
    tm = _pick(t, (512, 256, 128))
    return pl.pallas_call(
        _pair_rmsnorm_kernel,
        grid=(t // tm,),
        in_specs=[pl.BlockSpec((tm, wa), lambda i: (i, 0)), pl.BlockSpec((tm, wb), lambda i: (i, 0)),
                  pl.BlockSpec((1, wa), lambda i: (0, 0)), pl.BlockSpec((1, wb), lambda i: (0, 0))],
        out_specs=pl.BlockSpec((tm, wa + wb), lambda i: (i, 0)),
        out_shape=jax.ShapeDtypeStruct((t, wa + wb), BF16),
        compiler_params=_cparams(("parallel",)),
    )(a, b, ga.reshape(1, wa).astype(F32), gb.reshape(1, wb).astype(F32))


def _mm_kernel(*refs, nk, swiglu, has_cscale, has_res):
    it = iter(refs)
    x_ref = next(it)
    w_ref = next(it)
    w2_ref = next(it) if swiglu else None
    cs_ref = next(it) if has_cscale else None
    res_ref = next(it) if has_res else None
    o_ref = next(it)
    acc_ref = next(it) if nk > 1 else None
    acc2_ref = next(it) if (nk > 1 and swiglu) else None

    x = x_ref[...]

    def epilogue(r, r2):
        if swiglu:
            r = (r * jax.nn.sigmoid(r)) * r2
        if has_cscale:
            r = r * cs_ref[...]
        if has_res:
            r = res_ref[...] + r
        o_ref[...] = r.astype(o_ref.dtype)

    if nk == 1:
        r = jnp.dot(x, w_ref[...], preferred_element_type=F32)
        r2 = jnp.dot(x, w2_ref[...], preferred_element_type=F32) if swiglu else None
        epilogue(r, r2)
        return

    k = pl.program_id(2)

    def accum(acc, w):
        part = jnp.dot(x, w[...], preferred_element_type=F32)

        @pl.when(k == 0)
        def _():
            acc[...] = part

        @pl.when(k > 0)
        def _():
            acc[...] += part

    accum(acc_ref, w_ref)
    if swiglu:
        accum(acc2_ref, w2_ref)

    @pl.when(k == nk - 1)
    def _():
        epilogue(acc_ref[...], acc2_ref[...] if swiglu else None)


def _matmul(x, w, *, w2=None, cscale=None, res=None, out_dtype=F32, tm=1024, tn=512, tk=None):
    m, kdim = x.shape
    n = w.shape[1]
    tm = _pick(m, (tm, 512, 256, 128))
    tn = _pick(n, (tn, 512, 256, 128))
    tk = kdim if tk is None else tk
    assert kdim % tk == 0
    nk = kdim // tk
    swiglu = w2 is not None
    in_specs = [pl.BlockSpec((tm, tk), lambda i, j, k: (i, k)),
                pl.BlockSpec((tk, tn), lambda i, j, k: (k, j))]
    args = [x, w]
    if swiglu:
        in_specs.append(pl.BlockSpec((tk, tn), lambda i, j, k: (k, j)))
        args.append(w2)
    if cscale is not None:
        in_specs.append(pl.BlockSpec((1, tn), lambda i, j, k: (0, j)))
        args.append(cscale)
    if res is not None:
        in_specs.append(pl.BlockSpec((tm, tn), lambda i, j, k: (i, j)))
        args.append(res)
    scratch = []
    if nk > 1:
        scratch = [pltpu.VMEM((tm, tn), F32)] * (2 if swiglu else 1)
    kern = functools.partial(_mm_kernel, nk=nk, swiglu=swiglu, has_cscale=cscale is not None,
                             has_res=res is not None)
    return pl.pallas_call(
        kern,
        grid=(m // tm, n // tn, nk),
        in_specs=in_specs,
        out_specs=pl.BlockSpec((tm, tn), lambda i, j, k: (i, j)),
        out_shape=jax.ShapeDtypeStruct((m, n), out_dtype),
        scratch_shapes=scratch,
        compiler_params=_cparams(("parallel", "parallel", "arbitrary")),
    )(*args)


def _cumsum_kernel(f_ref, b_ref, o_ref, carry_ref):
    j = pl.program_id(1)

    @pl.when(j == 0)
    def _():
        carry_ref[...] = jnp.zeros_like(carry_ref)

    z = f_ref[...] + b_ref[...]
    logf = jnp.minimum(z, 0.0) - jnp.log1p(jnp.exp(-jnp.abs(z)))
    ts = z.shape[0]
    row = lax.broadcasted_iota(jnp.int32, (ts, ts), 0)
    col = lax.broadcasted_iota(jnp.int32, (ts, ts), 1)
    tri = jnp.where(col <= row, 1.0, 0.0).astype(F32)
    cum = jnp.dot(tri, logf, preferred_element_type=F32, precision=lax.Precision.HIGHEST)
    cum = cum + carry_ref[0:1, :]
    carry_ref[...] = jnp.broadcast_to(cum[ts - 1:ts, :], carry_ref.shape)
    o_ref[...] = cum


def _forget_cumsum(f_raw, f_bias):
    b, s, _ = f_raw.shape
    ts = _pick(s, (256, 128))
    return pl.pallas_call(
        _cumsum_kernel,
        grid=(b, s // ts),
        in_specs=[pl.BlockSpec((None, ts, LANES), lambda bi, j: (bi, j, 0)),
                  pl.BlockSpec((1, LANES), lambda bi, j: (0, 0))],
        out_specs=pl.BlockSpec((None, ts, LANES), lambda bi, j: (bi, j, 0)),
        out_shape=jax.ShapeDtypeStruct((b, s, LANES), F32),
        scratch_shapes=[pltpu.VMEM((8, LANES), F32)],
        compiler_params=_cparams(("parallel", "arbitrary")),
    )(f_raw, f_bias)


def _online_steps(tiles, m_ref, acc_ref):
    m = m_ref[...]
    acc = acc_ref[...]
    for st, vt in tiles:
        m_new = jnp.maximum(m, jnp.max(st, axis=0, keepdims=True))
        alpha = jnp.exp2(m - m_new)
        p = jnp.exp2(st - m_new).astype(BF16)
        acc = alpha * acc + jnp.dot(vt, p, preferred_element_type=F32)
        m = m_new
    m_ref[...] = m
    acc_ref[...] = acc


def _softmax_init(m_ref, acc_ref):
    m_ref[...] = jnp.full_like(m_ref, NEG_INF)
    acc_ref[...] = jnp.zeros_like(acc_ref)


def _normalized(acc):
    return acc[:HEAD_DIM] / acc[HEAD_DIM:HEAD_DIM + 1]


def _with_ones_rows(vt):
    ones = jnp.ones(vt.shape[:-2] + (ONES_ROWS, vt.shape[-1]), vt.dtype)
    return jnp.concatenate([vt, ones], axis=-2)


def _lane_concat(blocks, n):
    return blocks[0] if n == 1 else jnp.concatenate([blocks[j] for j in range(n)], axis=1)


def _fox_kernel(q_ref, k_ref, vt_ref, c_ref, o_ref, m_ref, acc_ref, ck_ref, *, tq):
    qi = pl.program_id(2)
    nq = c_ref.shape[0]

    @pl.when(qi == 0)
    def _():
        for j in range(nq):
            ck_ref[j * tq:(j + 1) * tq, :] = jnp.broadcast_to(c_ref[j:j + 1, :], (LANES, tq)).T

    q = q_ref[...]
    cq = c_ref[pl.ds(qi, 1), :]
    _softmax_init(m_ref, acc_ref)

    def step(ki, diagonals):
        tiles = []
        for sub, diagonal in enumerate(diagonals):
            start = pl.multiple_of((ki + sub) * tq, tq)
            st = lax.dot_general(k_ref[pl.ds(start, tq), :], q, _NT, preferred_element_type=F32)
            ck = ck_ref[pl.ds(start, tq), :]
            st = st + cq - jnp.concatenate([ck] * (tq // LANES), axis=1)
            if diagonal:
                key = lax.broadcasted_iota(jnp.int32, st.shape, 0)
                qry = lax.broadcasted_iota(jnp.int32, st.shape, 1)
                st = jnp.where(key <= qry, st, -jnp.inf)
            tiles.append((st, vt_ref[ki + sub]))
        _online_steps(tiles, m_ref, acc_ref)

    def pair(j, carry):
        step(2 * j, (False, False))
        return carry

    lax.fori_loop(0, qi // 2, pair, 0)

    @pl.when(qi % 2 == 1)
    def _():
        step(qi - 1, (False, True))

    @pl.when(qi % 2 == 0)
    def _():
        step(qi, (True,))

    o_ref[...] = _normalized(acc_ref[...]).T


def _fox_attention(proj3, q0, k0, v, cum):
    b, s, h, hd = v.shape
    tq = _pick(s, (512, 256, 128))
    nq = s // tq
    c2 = (cum * LOG2E).transpose(0, 2, 1).reshape(b, h, nq, tq)
    vt = _with_ones_rows(v.reshape(b, nq, tq, h, hd).transpose(0, 3, 1, 4, 2))
    hv = hd + ONES_ROWS
    return pl.pallas_call(
        functools.partial(_fox_kernel, tq=tq),
        grid=(b, h, nq),
        in_specs=[pl.BlockSpec((None, tq, hd), lambda bi, hi, qi: (bi, qi, q0 + hi)),
                  pl.BlockSpec((None, s, hd), lambda bi, hi, qi: (bi, 0, k0 + hi)),
                  pl.BlockSpec((None, None, nq, hv, tq), lambda bi, hi, qi: (bi, hi, 0, 0, 0)),
                  pl.BlockSpec((None, None, nq, tq), lambda bi, hi, qi: (bi, hi, 0, 0))],
        out_specs=pl.BlockSpec((None, tq, hd), lambda bi, hi, qi: (bi, qi, hi)),
        out_shape=jax.ShapeDtypeStruct((b, s, h * hd), F32),
        scratch_shapes=[pltpu.VMEM((1, tq), F32), pltpu.VMEM((hv, tq), F32), pltpu.VMEM((s, LANES), F32)],
        compiler_params=_cparams(("parallel", "parallel", "arbitrary")),
    )(proj3, proj3, vt, c2)


def _compress_kernel(r_ref, w1_ref, pe_ref, w2_ref, o_ref):
    ab = jnp.dot(r_ref[...], w1_ref[...], preferred_element_type=F32)
    nc = ab.shape[0]
    half = pe_ref.shape[1] // 2
    first = ab[:, :HEAD_DIM]
    second = pltpu.roll(ab[:, HEAD_DIM:], nc - 1, axis=0)
    pe_term = (jnp.dot(pe_ref[:, :half], w1_ref[:, :HEAD_DIM], preferred_element_type=F32)
               + jnp.dot(pe_ref[:, half:], w1_ref[:, HEAD_DIM:], preferred_element_type=F32))
    pre = first + second + pe_term[0:1, :]
    hid = pre * jax.nn.sigmoid(pre)
    o_ref[...] = jnp.dot(hid.astype(BF16), w2_ref[...], preferred_element_type=F32).astype(o_ref.dtype)


def _compress(rows, w1cat, pe_flat, w2):
    two, b, g, nc, width = rows.shape
    return pl.pallas_call(
        _compress_kernel,
        grid=(two, b, g),
        in_specs=[pl.BlockSpec((None, None, None, nc, width), lambda a, bi, gi: (a, bi, gi, 0, 0)),
                  pl.BlockSpec((None, width, 2 * HEAD_DIM), lambda a, bi, gi: (a, 0, 0)),
                  pl.BlockSpec((None, 8, 2 * width), lambda a, bi, gi: (a, 0, 0)),
                  pl.BlockSpec((None, HEAD_DIM, HEAD_DIM), lambda a, bi, gi: (a, 0, 0))],
        out_specs=pl.BlockSpec((None, None, None, nc, HEAD_DIM), lambda a, bi, gi: (a, bi, gi, 0, 0)),
        out_shape=jax.ShapeDtypeStruct((two, b, g, nc, HEAD_DIM), BF16),
        compiler_params=_cparams(("parallel", "parallel", "parallel")),
    )(rows, w1cat, pe_flat, w2)


def _stack_heads(q_ref):
    return jnp.concatenate([q_ref[:, r * HEAD_DIM:(r + 1) * HEAD_DIM] for r in range(NSA_GQA)], axis=0)


def _cmp_attn_kernel(q_ref, kc_ref, vc_ref, gd_ref, ov_ref, oc_ref, ns_ref, *, n_sel, top_n):
    i = pl.program_id(2)
    t0 = i * Q_BLOCK
    q4 = _stack_heads(q_ref)
    s = lax.dot_general(q4, kc_ref[...], _NT, preferred_element_type=F32)
    ncp = s.shape[1]
    shift = lax.rem(i * (Q_BLOCK // CMP_STRIDE) + ncp // 2, ncp)
    bias = pltpu.roll(gd_ref[...], shift, axis=1)
    t = t0 + (lax.broadcasted_iota(jnp.int32, s.shape, 0) & (Q_BLOCK - 1))
    cmp_end = lax.broadcasted_iota(jnp.int32, s.shape, 1) * CMP_STRIDE + (CMP_BLOCK - 1)
    s = jnp.where(t >= cmp_end, s + bias, -jnp.inf)
    m = jnp.maximum(jnp.max(s, axis=1, keepdims=True), NEG_INF)
    p = jnp.exp2(s - m)
    l = jnp.sum(p, axis=1, keepdims=True)
    p = (p * jnp.where(l > 0.0, 1.0 / l, 0.0)).astype(BF16)
    o = jnp.dot(p, vc_ref[...], preferred_element_type=F32)
    for r in range(NSA_GQA):
        oc_ref[:, r * HEAD_DIM:(r + 1) * HEAD_DIM] = o[r * Q_BLOCK:(r + 1) * Q_BLOCK]
    imp4 = jnp.dot(p, ov_ref[...], preferred_element_type=F32)
    imp = imp4[0:Q_BLOCK]
    for r in range(1, NSA_GQA):
        imp = imp + imp4[r * Q_BLOCK:(r + 1) * Q_BLOCK]

    shape = (LANES, Q_BLOCK)
    blk = lax.broadcasted_iota(jnp.int32, shape, 0).astype(F32)
    cur = ((t0 + lax.broadcasted_iota(jnp.int32, shape, 1)) // SEL_BLOCK).astype(F32)
    visible = blk <= cur
    forced = (blk == 0.0) | (blk == cur) | (blk == cur - 1.0)
    score = jnp.where(visible, imp.T + jnp.where(forced, FORCE_BONUS, 0.0), NEG_INF)
    score = jnp.where(blk < float(n_sel), score, REMOVED)

    def pick(_, carry):
        score, sel = carry
        mx = jnp.max(score, axis=0, keepdims=True)
        first = jnp.min(jnp.where(score == mx, blk, float(LANES)), axis=0, keepdims=True)
        hit = blk == first
        return jnp.where(hit, REMOVED, score), jnp.where(hit, 1.0, sel)

    _, sel = lax.fori_loop(0, top_n, pick, (score, jnp.zeros(shape, F32)))
    ns_ref[...] = jnp.where((sel > 0.0) & visible, 0.0, -MASK_BIG).T.astype(ns_ref.dtype)


def _cmp_attention(proj3, kv_cmp, gd, overlap, n_groups):
    b, s, _ = proj3.shape
    ncp = kv_cmp.shape[3]
    n_sel = s // SEL_BLOCK
    width = NSA_GQA * HEAD_DIM
    kern = functools.partial(_cmp_attn_kernel, n_sel=n_sel, top_n=min(SEL_TOPK, n_sel))
    return pl.pallas_call(
        kern,
        grid=(b, n_groups, s // Q_BLOCK),
        in_specs=[pl.BlockSpec((None, Q_BLOCK, width), lambda bi, g, i: (bi, i, g)),
                  pl.BlockSpec((None, None, None, ncp, HEAD_DIM), lambda bi, g, i: (0, bi, g, 0, 0)),
                  pl.BlockSpec((None, None, None, ncp, HEAD_DIM), lambda bi, g, i: (1, bi, g, 0, 0)),
                  pl.BlockSpec((None, NSA_GQA * Q_BLOCK, ncp), lambda bi, g, i: (g, 0, 0)),
                  pl.BlockSpec((ncp, LANES), lambda bi, g, i: (0, 0))],
        out_specs=[pl.BlockSpec((None, Q_BLOCK, width), lambda bi, g, i: (bi, i, g)),
                   pl.BlockSpec((None, None, Q_BLOCK, LANES), lambda bi, g, i: (bi, g, i, 0))],
        out_shape=[jax.ShapeDtypeStruct((b, s, n_groups * width), F32),
                   jax.ShapeDtypeStruct((b, n_groups, s, LANES), BF16)],
        compiler_params=_cparams(("parallel", "parallel", "parallel")),
    )(proj3, kv_cmp, kv_cmp, gd, overlap)


def _sel_win_kernel(q_ref, ka_ref, vst_ref, kw_ref, vwt_ref, ns_ref, oc_ref, g_ref, bn_ref, bw_ref,
                    o_ref, m_ref, acc_ref):
    i = pl.program_id(2)
    t0 = pl.multiple_of(i * Q_BLOCK, Q_BLOCK)
    q4 = _stack_heads(q_ref)

    ns = ns_ref[...]
    lane = lax.broadcasted_iota(jnp.int32, ns.shape, 1)
    near_blk = (lane >= 2 * i - 2) & (lane <= 2 * i + 1)
    ns_far = jnp.where(near_blk, -MASK_BIG, ns.astype(F32)).astype(BF16)
    qa_far = jnp.concatenate([q4, jnp.concatenate([ns_far] * NSA_GQA, axis=0)], axis=1)
    qa_near = jnp.concatenate([q4, jnp.concatenate([ns] * NSA_GQA, axis=0)], axis=1)
    _softmax_init(m_ref, acc_ref)

    blocks_per_tile = SEL_TK // Q_BLOCK

    def far_tile(j):
        start = pl.multiple_of(Q_BLOCK + j * SEL_TK, Q_BLOCK)
        st = lax.dot_general(ka_ref[pl.ds(start, SEL_TK), :], qa_far, _NT, preferred_element_type=F32)
        return st, _lane_concat(vst_ref[pl.ds(1 + j * blocks_per_tile, blocks_per_tile)], blocks_per_tile)

    def near_tile():
        st = lax.dot_general(ka_ref[pl.ds(t0, NEAR), :], qa_near, _NT, preferred_element_type=F32)
        st = st + bn_ref[...]
        key_pos = t0 - Q_BLOCK + lax.broadcasted_iota(jnp.int32, st.shape, 0)
        return jnp.where(key_pos >= 0, st, -jnp.inf), _lane_concat(vst_ref[pl.ds(i, 2)], 2)

    n_far = jnp.where(i == 0, 0, (i + 2) // 4)

    def far_pair(j, carry):
        _online_steps([far_tile(2 * j), far_tile(2 * j + 1)], m_ref, acc_ref)
        return carry

    lax.fori_loop(0, n_far // 2, far_pair, 0)

    @pl.when(n_far % 2 == 1)
    def _():
        _online_steps([far_tile(n_far - 1), near_tile()], m_ref, acc_ref)

    @pl.when(n_far % 2 == 0)
    def _():
        _online_steps([near_tile()], m_ref, acc_ref)

    o_sel = _normalized(acc_ref[...])

    span = Q_BLOCK + WINDOW
    sw = lax.dot_general(kw_ref[pl.ds(t0, span), :], q4, _NT, preferred_element_type=F32)
    sw = sw + bw_ref[...]
    key_pos = t0 - WINDOW + lax.broadcasted_iota(jnp.int32, sw.shape, 0)
    sw = jnp.where(key_pos >= 0, sw, -jnp.inf)
    pw = jnp.exp2(sw - jnp.max(sw, axis=0, keepdims=True)).astype(BF16)
    n_blk = span // Q_BLOCK
    o_win = _normalized(jnp.dot(_lane_concat(vwt_ref[pl.ds(i, n_blk)], n_blk), pw,
                                preferred_element_type=F32))

    gate = jax.nn.sigmoid(g_ref[...])
    gate_t = gate.T
    for r in range(NSA_GQA):
        lo, hi = r * Q_BLOCK, (r + 1) * Q_BLOCK
        mix_t = gate_t[3 * r + 1:3 * r + 2, :] * o_sel[:, lo:hi] + gate_t[3 * r + 2:3 * r + 3, :] * o_win[:, lo:hi]
        o_ref[:, lo:hi] = gate[:, 3 * r:3 * r + 1] * oc_ref[:, lo:hi] + mix_t.T


def _sel_win_attention(proj3, k_aug, vt_sel, k_win, vt_win, negsel, o_cmp, gates, bias_near, bias_win, n_groups):
    b, s, _ = proj3.shape
    width = NSA_GQA * HEAD_DIM
    rows = NSA_GQA * Q_BLOCK
    sp_sel = k_aug.shape[1]
    sp_win = k_win.shape[1]
    nb_sel = vt_sel.shape[2]
    nb_win = vt_win.shape[2]
    return pl.pallas_call(
        _sel_win_kernel,
        grid=(b, n_groups, s // Q_BLOCK),
        in_specs=[pl.BlockSpec((None, Q_BLOCK, width), lambda bi, g, i: (bi, i, g)),
                  pl.BlockSpec((None, sp_sel, 2 * HEAD_DIM), lambda bi, g, i: (bi, 0, g)),
                  pl.BlockSpec((None, None, nb_sel, HEAD_DIM + ONES_ROWS, Q_BLOCK), lambda bi, g, i: (bi, g, 0, 0, 0)),
                  pl.BlockSpec((None, sp_win, HEAD_DIM), lambda bi, g, i: (bi, 0, g)),
                  pl.BlockSpec((None, None, nb_win, HEAD_DIM + ONES_ROWS, Q_BLOCK), lambda bi, g, i: (bi, g, 0, 0, 0)),
                  pl.BlockSpec((None, None, Q_BLOCK, LANES), lambda bi, g, i: (bi, g, i, 0)),
                  pl.BlockSpec((None, Q_BLOCK, width), lambda bi, g, i: (bi, i, g)),
                  pl.BlockSpec((None, Q_BLOCK, LANES), lambda bi, g, i: (bi, i, g)),
                  pl.BlockSpec((None, NEAR, rows), lambda bi, g, i: (g, 0, 0)),
                  pl.BlockSpec((None, Q_BLOCK + WINDOW, rows), lambda bi, g, i: (g, 0, 0))],
        out_specs=pl.BlockSpec((None, Q_BLOCK, width), lambda bi, g, i: (bi, i, g)),
        out_shape=jax.ShapeDtypeStruct((b, s, n_groups * width), F32),
        scratch_shapes=[pltpu.VMEM((1, rows), F32), pltpu.VMEM((HEAD_DIM + ONES_ROWS, rows), F32)],
        compiler_params=_cparams(("parallel", "parallel", "arbitrary")),
    )(proj3, k_aug, vt_sel, k_win, vt_win, negsel, o_cmp, gates, bias_near, bias_win)


def _router_kernel(h_ref, w_ref, o_ref, *, n_experts):
    logits = jnp.dot(h_ref[...], w_ref[...], preferred_element_type=F32)
    lane = lax.broadcasted_iota(jnp.int32, logits.shape, 1).astype(F32)
    logits = jnp.where(lane < float(n_experts), logits, -jnp.inf)
    m1 = jnp.max(logits, axis=1, keepdims=True)
    i1 = jnp.min(jnp.where(logits == m1, lane, float(LANES)), axis=1, keepdims=True)
    rest = jnp.where(lane == i1, -jnp.inf, logits)
    m2 = jnp.max(rest, axis=1, keepdims=True)
    i2 = jnp.min(jnp.where(rest == m2, lane, float(LANES)), axis=1, keepdims=True)
    e2 = jnp.exp(m2 - m1)
    denom = 1.0 + e2
    o_ref[...] = (jnp.where(lane == 0.0, i1, 0.0) + jnp.where(lane == 1.0, i2, 0.0)
                  + jnp.where(lane == 2.0, 1.0 / denom, 0.0) + jnp.where(lane == 3.0, e2 / denom, 0.0))


def _router(h, w_router_padded, n_experts):
    t, d = h.shape
    tm = _pick(t, (512, 256, 128))
    return pl.pallas_call(
        functools.partial(_router_kernel, n_experts=n_experts),
        grid=(t // tm,),
        in_specs=[pl.BlockSpec((tm, d), lambda i: (i, 0)), pl.BlockSpec((d, LANES), lambda i: (0, 0))],
        out_specs=pl.BlockSpec((tm, LANES), lambda i: (i, 0)),
        out_shape=jax.ShapeDtypeStruct((t, LANES), F32),
        compiler_params=_cparams(("parallel",)),
    )(h, w_router_padded)


def _row_copy(src_ref, dst_ref, sem, src_row, dst_row):
    return pltpu.make_async_copy(src_ref.at[pl.ds(src_row, 1)], dst_ref.at[pl.ds(dst_row, 1)], sem)


def _gather_rows_kernel(idx_ref, src_ref, out_ref, sem, *, rows):
    base = pl.program_id(0) * rows

    def start(r2, carry):
        for lane in range(2):
            r = 2 * r2 + lane
            _row_copy(src_ref, out_ref, sem, idx_ref[base + r], r).start(priority=lane)
        return carry

    lax.fori_loop(0, rows // 2, start, 0)

    def wait(r, carry):
        _row_copy(src_ref, out_ref, sem, 0, r).wait()
        return carry

    lax.fori_loop(0, rows, wait, 0)


def _gather_rows(src, idx):
    n_out = idx.shape[0]
    n_src, width = src.shape
    rows = _pick(n_out, (GATHER_ROWS, 256, 128))
    src = src.reshape(n_src, width // LANES, LANES)
    out = pl.pallas_call(
        functools.partial(_gather_rows_kernel, rows=rows),
        grid_spec=pltpu.PrefetchScalarGridSpec(
            num_scalar_prefetch=1,
            grid=(n_out // rows,),
            in_specs=[pl.BlockSpec(memory_space=pl.ANY)],
            out_specs=pl.BlockSpec((rows,) + src.shape[1:], lambda c, idx_ref: (c, 0, 0)),
            scratch_shapes=[pltpu.SemaphoreType.DMA(())]),
        out_shape=jax.ShapeDtypeStruct((n_out,) + src.shape[1:], src.dtype),
        compiler_params=_cparams(("arbitrary",)),
    )(idx, src)
    return out.reshape(n_out, width)


def _gmm_kernel(te_ref, nv_ref, x_ref, w_ref, *rest, swiglu):
    o_ref = rest[-1]
    live = pl.program_id(1) < nv_ref[0]

    @pl.when(live)
    def _():
        x = x_ref[...]
        r = jnp.dot(x, w_ref[...].astype(BF16), preferred_element_type=F32)
        if swiglu:
            r = (r * jax.nn.sigmoid(r)) * jnp.dot(x, rest[0][...].astype(BF16), preferred_element_type=F32)
        o_ref[...] = r.astype(o_ref.dtype)

    @pl.when(jnp.logical_not(live))
    def _():
        o_ref[...] = jnp.zeros_like(o_ref)


def _grouped_matmul(x, w, tile_expert, n_valid, *, w2=None, tn=512):
    p, kdim = x.shape
    n = w.shape[2]
    tn = _pick(n, (tn, 256, 128))
    wspec = pl.BlockSpec((None, kdim, tn), lambda j, i, te, nv: (te[i], 0, j))
    in_specs = [pl.BlockSpec((MOE_TM, kdim), lambda j, i, te, nv: (i, 0)), wspec]
    args = [x, w]
    if w2 is not None:
        in_specs.append(wspec)
        args.append(w2)
    return pl.pallas_call(
        functools.partial(_gmm_kernel, swiglu=w2 is not None),
        grid_spec=pltpu.PrefetchScalarGridSpec(
            num_scalar_prefetch=2,
            grid=(n // tn, p // MOE_TM),
            in_specs=in_specs,
            out_specs=pl.BlockSpec((MOE_TM, tn), lambda j, i, te, nv: (i, j))),
        out_shape=jax.ShapeDtypeStruct((p, n), BF16),
        compiler_params=_cparams(("parallel", "parallel")),
    )(tile_expert, n_valid, *args)


def _combine_kernel(x_ref, ya_ref, yb_ref, r_ref, o_ref):
    route = r_ref[...]
    o_ref[...] = (x_ref[...] + route[:, 2:3] * ya_ref[...].astype(F32)
                  + route[:, 3:4] * yb_ref[...].astype(F32))


def _moe_combine(x, y2, route):
    t, d = x.shape
    tm = _pick(t, (256, 128))
    nt = t // tm
    return pl.pallas_call(
        _combine_kernel,
        grid=(nt,),
        in_specs=[pl.BlockSpec((tm, d), lambda i: (i, 0)),
                  pl.BlockSpec((tm, d), lambda i: (i, 0)),
                  pl.BlockSpec((tm, d), lambda i: (i + nt, 0)),
                  pl.BlockSpec((tm, LANES), lambda i: (i, 0))],
        out_specs=pl.BlockSpec((tm, d), lambda i: (i, 0)),
        out_shape=jax.ShapeDtypeStruct((t, d), F32),
        compiler_params=_cparams(("parallel",)),
    )(x, y2, y2, route)


def _route_plan(i1, i2, n_experts):
    t = i1.shape[0]
    e_flat = jnp.concatenate([i1, i2])
    onehot = (e_flat[:, None] == jnp.arange(n_experts, dtype=jnp.int32)[None, :]).astype(jnp.int32)
    csum = jnp.cumsum(onehot, axis=0)
    counts = csum[-1]
    rank = jnp.sum((csum - onehot) * onehot, axis=1)
    padded = ((counts + MOE_TM - 1) // MOE_TM) * MOE_TM
    ends = jnp.cumsum(padded)
    pos = jnp.sum(onehot * (ends - padded)[None, :], axis=1) + rank
    p_rows = 2 * t + n_experts * MOE_TM
    token = jnp.tile(jnp.arange(t, dtype=jnp.int32), 2)
    src = jnp.zeros((p_rows,), jnp.int32).at[pos].set(token)
    tile_start = jnp.arange(p_rows // MOE_TM, dtype=jnp.int32) * MOE_TM
    tile_expert = jnp.minimum(jnp.sum((tile_start[:, None] >= ends[None, :]).astype(jnp.int32), axis=1),
                              n_experts - 1)
    n_valid = (ends[-1] // MOE_TM).reshape(1)
    return pos.astype(jnp.int32), src, tile_expert.astype(jnp.int32), n_valid.astype(jnp.int32)


def _moe_ffn(x, h, router_w, wg, wu, wd):
    n_experts = router_w.shape[1]
    route = _router(h, jnp.pad(router_w, ((0, 0), (0, LANES - n_experts))).astype(BF16), n_experts)
    i1 = route[:, 0].astype(jnp.int32)
    i2 = route[:, 1].astype(jnp.int32)
    pos, src, tile_expert, n_valid = _route_plan(i1, i2, n_experts)
    xs = _gather_rows(h, src)
    act = _grouped_matmul(xs, wg, tile_expert, n_valid, w2=wu)
    y = _grouped_matmul(act, wd, tile_expert, n_valid)
    return _moe_combine(x, _gather_rows(y, pos), route)


def _static_tables(s):
    n_sel = s // SEL_BLOCK
    ncp = max(LANES, -(-(s // CMP_STRIDE) // LANES) * LANES)
    n_cmp = (s - CMP_BLOCK) // CMP_STRIDE + 1
    span = Q_BLOCK + WINDOW
    d_near = (NEAR - 1) - np.arange(NEAR + Q_BLOCK - 1)
    d_win = (span - 1) - np.arange(span + Q_BLOCK - 1)
    r = np.arange(Q_BLOCK)[:, None]
    d_cmp = r - CMP_STRIDE * np.arange(CMP_NEAR[0], CMP_NEAR[1])[None, :] - (CMP_BLOCK - 1)
    c0 = np.arange(ncp)[:, None] * CMP_STRIDE
    j0 = np.arange(LANES)[None, :] * SEL_BLOCK
    overlap = np.clip(np.minimum(c0 + CMP_BLOCK, j0 + SEL_BLOCK) - np.maximum(c0, j0), 0, None).astype(np.float32) / CMP_BLOCK
    overlap[n_cmp:, :] = 0.0
    overlap[:, n_sel:] = 0.0
    onehot = (np.arange(s)[:, None] // SEL_BLOCK == np.arange(LANES)[None, :]).astype(np.float32)
    return dict(
        ncp=ncp,
        b_near=_bucket_np(d_near), ok_near=d_near >= 0,
        b_win=_bucket_np(d_win), ok_win=(d_win >= 0) & (d_win < WINDOW),
        b_cmp=_bucket_np(d_cmp), ok_cmp=(d_cmp >= 0) & (d_cmp < FAR_DIST),
        overlap=overlap, onehot=onehot)


def _bias_tables(rel_bias, tabs, n_groups):
    tbl = rel_bias.T.astype(F32) * LOG2E
    reb = tbl - tbl[:, FAR_BUCKET:FAR_BUCKET + 1]
    heads = reb.shape[0]

    def toeplitz(bucket, ok, width):
        seq = jnp.where(jnp.asarray(ok)[None, :], reb[:, jnp.asarray(bucket)], -MASK_BIG)
        period = width + Q_BLOCK
        seq = jnp.pad(seq, ((0, 0), (0, period - seq.shape[1])))
        rolled = jnp.tile(seq, (1, Q_BLOCK + 1))[:, :Q_BLOCK * (period + 1)].reshape(heads, Q_BLOCK, period + 1)
        tile = rolled[:, ::-1, :width]
        tile = tile.reshape(n_groups, NSA_GQA, Q_BLOCK, width).transpose(0, 3, 1, 2)
        return tile.reshape(n_groups, width, NSA_GQA * Q_BLOCK)

    ncp = tabs["ncp"]
    win = jnp.where(jnp.asarray(tabs["ok_cmp"])[None], reb[:, jnp.asarray(tabs["b_cmp"])], 0.0)
    lo = ncp // 2 + CMP_NEAR[0]
    cmp_tab = jnp.pad(win, ((0, 0), (0, 0), (lo, ncp - lo - win.shape[-1])))
    return (toeplitz(tabs["b_near"], tabs["ok_near"], NEAR),
            toeplitz(tabs["b_win"], tabs["ok_win"], Q_BLOCK + WINDOW),
            cmp_tab.reshape(n_groups, NSA_GQA * Q_BLOCK, ncp))


def _key_blocks_t(a, g, front_blocks):
    b, s, _ = a.shape
    a = _with_ones_rows(a.reshape(b, s // Q_BLOCK, Q_BLOCK, g, HEAD_DIM).transpose(0, 3, 1, 4, 2))
    return jnp.pad(a, ((0, 0), (0, 0), (front_blocks, 0), (0, 0), (0, 0)))


def _mixer(h, w_in, f_bias, cmp_pe, cmp_w1, cmp_w2, biases, tabs, b, s):
    d = h.shape[1]
    n_heads = d // HEAD_DIM
    hn = n_heads // 2
    hf = n_heads - hn
    g = hn // NSA_GQA
    gw = g * HEAD_DIM
    qn_w, fox_w = hn * HEAD_DIM, hf * HEAD_DIM
    splits = (qn_w, gw, gw, gw, gw, gw, gw, hn * 3, fox_w, fox_w, fox_w, hf)
    offs = np.concatenate([[0], np.cumsum(splits)])
    col = lambda k: w_in[:, offs[k]:offs[k + 1]]
    scale = HEAD_DIM ** -0.5 * LOG2E

    w_main = jnp.concatenate([col(k) for k in (0, 1, 2, 3, 4, 5, 6, 8, 9, 10)], axis=1).astype(BF16)
    blocks = np.cumsum([0, hn, g, g, g, g, g, g, hf, hf, hf])
    cscale = np.ones((1, w_main.shape[1]), np.float32)
    cscale[0, :qn_w] = scale
    cscale[0, blocks[7] * HEAD_DIM:blocks[8] * HEAD_DIM] = scale
    proj = _matmul(h, w_main, cscale=jnp.asarray(cscale), out_dtype=BF16)
    proj3 = proj.reshape(b, s, -1)
    grp = lambda k: proj3[:, :, blocks[k] * HEAD_DIM:blocks[k + 1] * HEAD_DIM]

    w_gate = jnp.pad(col(7).reshape(d, g, NSA_GQA * 3), ((0, 0), (0, 0), (0, LANES - NSA_GQA * 3))).reshape(d, g * LANES)
    w_f = jnp.pad(col(11), ((0, 0), (0, LANES - hf)))
    small = _matmul(h, jnp.concatenate([w_gate, w_f], axis=1).astype(BF16), tn=(g + 1) * LANES)
    small3 = small.reshape(b, s, -1)
    gates = small3[:, :, :g * LANES]
    f_raw = small3[:, :, g * LANES:]

    cum = _forget_cumsum(f_raw, jnp.pad(f_bias.astype(F32), (0, LANES - hf)).reshape(1, LANES))
    o_fox = _fox_attention(proj3, int(blocks[7]), int(blocks[8]), grp(9).reshape(b, s, hf, HEAD_DIM),
                           cum[:, :, :hf])

    ncp = tabs["ncp"]

    def to_rows(a):
        a = a.reshape(b, s // CMP_STRIDE, CMP_STRIDE, g, HEAD_DIM).transpose(0, 3, 1, 2, 4)
        a = a.reshape(b, g, s // CMP_STRIDE, CMP_STRIDE * HEAD_DIM)
        return jnp.pad(a, ((0, 0), (0, 0), (0, ncp - s // CMP_STRIDE), (0, 0)))

    rows = jnp.stack([to_rows(grp(1)), to_rows(grp(2))])
    half = CMP_STRIDE * HEAD_DIM
    w1cat = jnp.concatenate([cmp_w1[:, :half], cmp_w1[:, half:]], axis=2).astype(BF16)
    pe_flat = jnp.pad(cmp_pe.reshape(2, 1, CMP_BLOCK * HEAD_DIM), ((0, 0), (0, 7), (0, 0))).astype(BF16)
    kv_cmp = _compress(rows, w1cat, pe_flat, cmp_w2.astype(BF16))

    bias_near, bias_win, bias_cmp = biases
    o_cmp, negsel = _cmp_attention(proj3, kv_cmp, bias_cmp, jnp.asarray(tabs["overlap"], BF16), g)

    onehot = jnp.broadcast_to(jnp.asarray(tabs["onehot"], BF16)[None, :, None, :], (b, s, g, LANES))
    k_aug = jnp.concatenate([grp(3).reshape(b, s, g, HEAD_DIM), onehot], axis=-1).reshape(b, s, g * 2 * HEAD_DIM)
    front = lambda a, n: jnp.pad(a, ((0, 0), (n, 0), (0, 0)))
    o_nsa = _sel_win_attention(proj3, front(k_aug, Q_BLOCK), _key_blocks_t(grp(4), g, 1),
                               front(grp(5), WINDOW), _key_blocks_t(grp(6), g, WINDOW // Q_BLOCK),
                               negsel, o_cmp, gates, bias_near, bias_win, g)
    return o_nsa.reshape(b * s, qn_w), o_fox.reshape(b * s, fox_w)


def _pad_to(a, axis, mult):
    n = a.shape[axis]
    target = -(-n // mult) * mult
    if target == n:
        return a
    pad = [(0, 0)] * a.ndim
    pad[axis] = (0, target - n)
    return jnp.pad(a, pad)


def _dense_ffn(x, h, wg, wu, wd):
    ff_tile = 512
    wg = _pad_to(wg, 1, ff_tile).astype(BF16)
    wu = _pad_to(wu, 1, ff_tile).astype(BF16)
    wd = _pad_to(wd, 0, ff_tile).astype(BF16)
    act = _matmul(h, wg, w2=wu, out_dtype=BF16, tn=ff_tile)
    return _matmul(act, wd, res=x, tm=1024, tn=1024, tk=wd.shape[0] // 4)


def kernel(x, attn_norm, w_in, fgate_bias, cmp_pe, cmp_w1, cmp_w2, rel_bias, out_norm_nsa, out_norm_fox,
           w_out, ffn_norm, dense_w_gate, dense_w_up, dense_w_down, router_w, moe_w_gate, moe_w_up,
           moe_w_down, final_norm):
    b, s, d = x.shape
    depth = attn_norm.shape[0]
    n_groups = (d // HEAD_DIM // 2) // NSA_GQA
    assert s % SEL_TK == 0 and s // SEL_BLOCK <= LANES and d % (2 * NSA_GQA * HEAD_DIM) == 0
    assert (b * s) % MOE_TM == 0
    tabs = _static_tables(s)
    biases = _bias_tables(rel_bias, tabs, n_groups)
    xt = x.reshape(b * s, d)
    for layer in range(depth):
        h = _rmsnorm(xt, attn_norm[layer], BF16)
        o_nsa, o_fox = _mixer(h, w_in[layer], fgate_bias[layer], cmp_pe[layer], cmp_w1[layer], cmp_w2[layer],
                              biases, tabs, b, s)
        mixed = _pair_rmsnorm(o_nsa, o_fox, out_norm_nsa[layer], out_norm_fox[layer])
        xt = _matmul(mixed, w_out[layer].astype(BF16), res=xt)
        h = _rmsnorm(xt, ffn_norm[layer], BF16)
        i = layer // 2
        if layer % 2 == 0:
            xt = _dense_ffn(xt, h, dense_w_gate[i], dense_w_up[i], dense_w_down[i])
        else:
            xt = _moe_ffn(xt, h, router_w[i], moe_w_gate[i], moe_w_up[i], moe_w_down[i])
    return _rmsnorm(xt, final_norm, x.dtype).reshape(b, s, d)
```

```python
import functools
import math

import numpy as np
import jax
import jax.numpy as jnp
from jax import lax
from jax.experimental import pallas as pl
from jax.experimental.pallas import tpu as pltpu

F32 = jnp.float32
BF16 = jnp.bfloat16

HEAD_DIM = 128
NSA_GQA = 4
CMP_BLOCK = 32
CMP_STRIDE = 16
SEL_BLOCK = 64
SEL_TOPK = 16
WINDOW = 512
Q_BLOCK = 128
N_BUCKETS = 32
MAX_DISTANCE = 128
TOP_K = 2
EPS = 1e-6
NEG_INF = -1e30
FORCE_BONUS = 1e4

LANES = 128
VMEM_LIMIT = 56 * 1024 * 1024

LOG2E = 1.4426950408889634
MASK_BIG = 2.0 ** 101
REMOVED = -3.0e38
FAR_BUCKET = N_BUCKETS - 1
FAR_DIST = 113
SEL_TK = 512
NEAR = 2 * Q_BLOCK
CMP_NEAR = (-9, 7)
TILES_PER_TRIP = 4
ONES_ROWS = 16
MOE_TM = 512
GATHER_ROWS = 512

_NT = (((1,), (1,)), ((), ()))


def _cparams(sem):
    return pltpu.CompilerParams(dimension_semantics=sem, vmem_limit_bytes=VMEM_LIMIT)


def _bucket_np(d):
    max_exact = N_BUCKETS // 2
    dd = np.maximum(d, 0)
    ratio = np.log(np.maximum(dd, max_exact).astype(np.float32) / max_exact) / math.log(MAX_DISTANCE / max_exact)
    large = np.minimum(max_exact + (ratio * (N_BUCKETS - max_exact)).astype(np.int32), N_BUCKETS - 1)
    return np.where(dd < max_exact, dd, large).astype(np.int32)


def _pick(n, prefs):
    for p in prefs:
        if n % p == 0:
            return p
    return n


def _rmsnorm_kernel(x_ref, g_ref, o_ref):
    x = x_ref[...].astype(F32)
    y = x * lax.rsqrt(jnp.mean(x * x, axis=-1, keepdims=True) + EPS)
    o_ref[...] = (y * g_ref[...]).astype(o_ref.dtype)


def _rmsnorm(x, gain, out_dtype):
    t, d = x.shape
    tm = _pick(t, (512, 256, 128))
    return pl.pallas_call(
        _rmsnorm_kernel,
        grid=(t // tm,),
        in_specs=[pl.BlockSpec((tm, d), lambda i: (i, 0)), pl.BlockSpec((1, d), lambda i: (0, 0))],
        out_specs=pl.BlockSpec((tm, d), lambda i: (i, 0)),
        out_shape=jax.ShapeDtypeStruct((t, d), out_dtype),
        compiler_params=_cparams(("parallel",)),
    )(x, gain.reshape(1, d).astype(F32))


def _pair_rmsnorm_kernel(a_ref, b_ref, ga_ref, gb_ref, o_ref):
    wa = a_ref.shape[-1]
    for ref, g, lo in ((a_ref, ga_ref, 0), (b_ref, gb_ref, wa)):
        x = ref[...].astype(F32)
        y = x * lax.rsqrt(jnp.mean(x * x, axis=-1, keepdims=True) + EPS)
        o_ref[:, lo:lo + x.shape[-1]] = (y * g[...]).astype(o_ref.dtype)


def _pair_rmsnorm(a, b, ga, gb):
    t, wa = a.shape
    wb = b.shape[1]
    tm = _pick(t, (512, 256, 128))
    return pl.pallas_call(
        _pair_rmsnorm_kernel,
        grid=(t // tm,),
        in_specs=[pl.BlockSpec((tm, wa), lambda i: (i, 0)), pl.BlockSpec((tm, wb), lambda i: (i, 0)),
                  pl.BlockSpec((1, wa), lambda i: (0, 0)), pl.BlockSpec((1, wb), lambda i: (0, 0))],
        out_specs=pl.BlockSpec((tm, wa + wb), lambda i: (i, 0)),
        out_shape=jax.ShapeDtypeStruct((t, wa + wb), BF16),
        compiler_params=_cparams(("parallel",)),
    )(a, b, ga.reshape(1, wa).astype(F32), gb.reshape(1, wb).astype(F32))


def _mm_kernel(*refs, nk, swiglu, has_cscale, has_res):
    it = iter(refs)
    x_ref = next(it)
    w_ref = next(it)
    w2_ref = next(it) if swiglu else None
    cs_ref = next(it) if has_cscale else None
    res_ref = next(it) if has_res else None
    o_ref = next(it)
    acc_ref = next(it) if nk > 1 else None
    acc2_ref = next(it) if (nk > 1 and swiglu) else None

    x = x_ref[...]

    def epilogue(r, r2):
        if swiglu:
            r = (r * jax.nn.sigmoid(r)) * r2
        if has_cscale:
            r = r * cs_ref[...]
        if has_res:
            r = res_ref[...] + r
        o_ref[...] = r.astype(o_ref.dtype)

    if nk == 1:
        r = jnp.dot(x, w_ref[...], preferred_element_type=F32)
        r2 = jnp.dot(x, w2_ref[...], preferred_element_type=F32) if swiglu else None
        epilogue(r, r2)
        return

    k = pl.program_id(2)

    def accum(acc, w):
        part = jnp.dot(x, w[...], preferred_element_type=F32)

        @pl.when(k == 0)
        def _():
            acc[...] = part

        @pl.when(k > 0)
        def _():
            acc[...] += part

    accum(acc_ref, w_ref)
    if swiglu:
        accum(acc2_ref, w2_ref)

    @pl.when(k == nk - 1)
    def _():
        epilogue(acc_ref[...], acc2_ref[...] if swiglu else None)


def _matmul(x, w, *, w2=None, cscale=None, res=None, out_dtype=F32, tm=1024, tn=512, tk=None):
    m, kdim = x.shape
    n = w.shape[1]
    tm = _pick(m, (tm, 512, 256, 128))
    tn = _pick(n, (tn, 512, 256, 128))
    tk = kdim if tk is None else tk
    assert kdim % tk == 0
    nk = kdim // tk
    swiglu = w2 is not None
    in_specs = [pl.BlockSpec((tm, tk), lambda i, j, k: (i, k)),
                pl.BlockSpec((tk, tn), lambda i, j, k: (k, j))]
    args = [x, w]
    if swiglu:
        in_specs.append(pl.BlockSpec((tk, tn), lambda i, j, k: (k, j)))
        args.append(w2)
    if cscale is not None:
        in_specs.append(pl.BlockSpec((1, tn), lambda i, j, k: (0, j)))
        args.append(cscale)
    if res is not None:
        in_specs.append(pl.BlockSpec((tm, tn), lambda i, j, k: (i, j)))
        args.append(res)
    scratch = []
    if nk > 1:
        scratch = [pltpu.VMEM((tm, tn), F32)] * (2 if swiglu else 1)
    kern = functools.partial(_mm_kernel, nk=nk, swiglu=swiglu, has_cscale=cscale is not None,
                             has_res=res is not None)
    return pl.pallas_call(
        kern,
        grid=(m // tm, n // tn, nk),
        in_specs=in_specs,
        out_specs=pl.BlockSpec((tm, tn), lambda i, j, k: (i, j)),
        out_shape=jax.ShapeDtypeStruct((m, n), out_dtype),
        scratch_shapes=scratch,
        compiler_params=_cparams(("parallel", "parallel", "arbitrary")),
    )(*args)


def _cumsum_kernel(f_ref, b_ref, o_ref, carry_ref):
    j = pl.program_id(1)

    @pl.when(j == 0)
    def _():
        carry_ref[...] = jnp.zeros_like(carry_ref)

    z = f_ref[...] + b_ref[...]
    logf = jnp.minimum(z, 0.0) - jnp.log1p(jnp.exp(-jnp.abs(z)))
    ts = z.shape[0]
    row = lax.broadcasted_iota(jnp.int32, (ts, ts), 0)
    col = lax.broadcasted_iota(jnp.int32, (ts, ts), 1)
    tri = jnp.where(col <= row, 1.0, 0.0).astype(F32)
    cum = jnp.dot(tri, logf, preferred_element_type=F32, precision=lax.Precision.HIGHEST)
    cum = cum + carry_ref[0:1, :]
    carry_ref[...] = jnp.broadcast_to(cum[ts - 1:ts, :], carry_ref.shape)
    o_ref[...] = cum


def _forget_cumsum(f_raw, f_bias):
    b, s, _ = f_raw.shape
    ts = _pick(s, (256, 128))
    return pl.pallas_call(
        _cumsum_kernel,
        grid=(b, s // ts),
        in_specs=[pl.BlockSpec((None, ts, LANES), lambda bi, j: (bi, j, 0)),
                  pl.BlockSpec((1, LANES), lambda bi, j: (0, 0))],
        out_specs=pl.BlockSpec((None, ts, LANES), lambda bi, j: (bi, j, 0)),
        out_shape=jax.ShapeDtypeStruct((b, s, LANES), F32),
        scratch_shapes=[pltpu.VMEM((8, LANES), F32)],
        compiler_params=_cparams(("parallel", "arbitrary")),
    )(f_raw, f_bias)


def _online_steps(tiles, m_ref, acc_ref, query_bias=None):
    m = m_ref[...]
    acc = acc_ref[...]
    for st, vt in tiles:
        peak = jnp.max(st, axis=0, keepdims=True)
        if query_bias is not None:
            peak = peak + query_bias
        m_new = jnp.maximum(m, peak)
        alpha = jnp.exp2(m - m_new)
        shift = m_new if query_bias is None else m_new - query_bias
        p = jnp.exp2(st - shift).astype(BF16)
        acc = alpha * acc + jnp.dot(vt, p, preferred_element_type=F32)
        m = m_new
    m_ref[...] = m
    acc_ref[...] = acc


def _softmax_init(m_ref, acc_ref):
    m_ref[...] = jnp.full_like(m_ref, NEG_INF)
    acc_ref[...] = jnp.zeros_like(acc_ref)


def _normalized(acc):
    return acc[:HEAD_DIM] / acc[HEAD_DIM:HEAD_DIM + 1]


def _with_ones_rows(vt):
    ones = jnp.ones(vt.shape[:-2] + (ONES_ROWS, vt.shape[-1]), vt.dtype)
    return jnp.concatenate([vt, ones], axis=-2)


def _lane_concat(blocks, n):
    return blocks[0] if n == 1 else jnp.concatenate([blocks[j] for j in range(n)], axis=1)


def _fox_kernel(q_ref, k_ref, vt_ref, c_ref, o_ref, m_ref, acc_ref, ck_ref, *, tq):
    qi = pl.program_id(2)
    nq = c_ref.shape[0]

    @pl.when(qi == 0)
    def _():
        for j in range(nq):
            ck_ref[j * tq:(j + 1) * tq, :] = jnp.broadcast_to(c_ref[j:j + 1, :], (LANES, tq)).T

    q = q_ref[...]
    cq = c_ref[pl.ds(qi, 1), :]
    _softmax_init(m_ref, acc_ref)

    def step(ki, diagonals):
        tiles = []
        for sub, diagonal in enumerate(diagonals):
            start = pl.multiple_of((ki + sub) * tq, tq)
            st = lax.dot_general(k_ref[pl.ds(start, tq), :], q, _NT, preferred_element_type=F32)
            ck = ck_ref[pl.ds(start, tq), :]
            st = st - jnp.concatenate([ck] * (tq // LANES), axis=1)
            if diagonal:
                key = lax.broadcasted_iota(jnp.int32, st.shape, 0)
                qry = lax.broadcasted_iota(jnp.int32, st.shape, 1)
                st = jnp.where(key <= qry, st, -jnp.inf)
            tiles.append((st, vt_ref[ki + sub]))
        _online_steps(tiles, m_ref, acc_ref, query_bias=cq)

    def quad(j, carry):
        step(TILES_PER_TRIP * j, (False,) * TILES_PER_TRIP)
        return carry

    n_quads = qi // TILES_PER_TRIP
    lax.fori_loop(0, n_quads, quad, 0)

    @pl.when(qi - TILES_PER_TRIP * n_quads >= 2)
    def _():
        step(TILES_PER_TRIP * n_quads, (False, False))

    @pl.when(qi % 2 == 1)
    def _():
        step(qi - 1, (False, True))

    @pl.when(qi % 2 == 0)
    def _():
        step(qi, (True,))

    o_ref[...] = _normalized(acc_ref[...]).T


def _fox_attention(proj3, q0, k0, v, cum):
    b, s, h, hd = v.shape
    tq = _pick(s, (512, 256, 128))
    nq = s // tq
    c2 = (cum * LOG2E).transpose(0, 2, 1).reshape(b, h, nq, tq)
    vt = _with_ones_rows(v.reshape(b, nq, tq, h, hd).transpose(0, 3, 1, 4, 2))
    hv = hd + ONES_ROWS
    return pl.pallas_call(
        functools.partial(_fox_kernel, tq=tq),
        grid=(b, h, nq),
        in_specs=[pl.BlockSpec((None, tq, hd), lambda bi, hi, qi: (bi, qi, q0 + hi)),
                  pl.BlockSpec((None, s, hd), lambda bi, hi, qi: (bi, 0, k0 + hi)),
                  pl.BlockSpec((None, None, nq, hv, tq), lambda bi, hi, qi: (bi, hi, 0, 0, 0)),
                  pl.BlockSpec((None, None, nq, tq), lambda bi, hi, qi: (bi, hi, 0, 0))],
        out_specs=pl.BlockSpec((None, tq, hd), lambda bi, hi, qi: (bi, qi, hi)),
        out_shape=jax.ShapeDtypeStruct((b, s, h * hd), F32),
        scratch_shapes=[pltpu.VMEM((1, tq), F32), pltpu.VMEM((hv, tq), F32), pltpu.VMEM((s, LANES), F32)],
        compiler_params=_cparams(("parallel", "parallel", "arbitrary")),
    )(proj3, proj3, vt, c2)


def _compress_kernel(r_ref, w1_ref, pe_ref, w2_ref, o_ref):
    ab = jnp.dot(r_ref[...], w1_ref[...], preferred_element_type=F32)
    nc = ab.shape[0]
    half = pe_ref.shape[1] // 2
    first = ab[:, :HEAD_DIM]
    second = pltpu.roll(ab[:, HEAD_DIM:], nc - 1, axis=0)
    pe_term = (jnp.dot(pe_ref[:, :half], w1_ref[:, :HEAD_DIM], preferred_element_type=F32)
               + jnp.dot(pe_ref[:, half:], w1_ref[:, HEAD_DIM:], preferred_element_type=F32))
    pre = first + second + pe_term[0:1, :]
    hid = pre * jax.nn.sigmoid(pre)
    o_ref[...] = jnp.dot(hid.astype(BF16), w2_ref[...], preferred_element_type=F32).astype(o_ref.dtype)


def _compress(rows, w1cat, pe_flat, w2):
    two, b, g, nc, width = rows.shape
    return pl.pallas_call(
        _compress_kernel,
        grid=(two, b, g),
        in_specs=[pl.BlockSpec((None, None, None, nc, width), lambda a, bi, gi: (a, bi, gi, 0, 0)),
                  pl.BlockSpec((None, width, 2 * HEAD_DIM), lambda a, bi, gi: (a, 0, 0)),
                  pl.BlockSpec((None, 8, 2 * width), lambda a, bi, gi: (a, 0, 0)),
                  pl.BlockSpec((None, HEAD_DIM, HEAD_DIM), lambda a, bi, gi: (a, 0, 0))],
        out_specs=pl.BlockSpec((None, None, None, nc, HEAD_DIM), lambda a, bi, gi: (a, bi, gi, 0, 0)),
        out_shape=jax.ShapeDtypeStruct((two, b, g, nc, HEAD_DIM), BF16),
        compiler_params=_cparams(("parallel", "parallel", "parallel")),
    )(rows, w1cat, pe_flat, w2)


def _stack_heads(q_ref):
    return jnp.concatenate([q_ref[:, r * HEAD_DIM:(r + 1) * HEAD_DIM] for r in range(NSA_GQA)], axis=0)


def _cmp_attn_kernel(q_ref, kc_ref, lhs_ref, gd_ref, oc_ref, ns_ref, *, n_sel, top_n):
    i = pl.program_id(2)
    t0 = i * Q_BLOCK
    q4 = _stack_heads(q_ref)
    st = lax.dot_general(kc_ref[...], q4, _NT, preferred_element_type=F32)
    ncp = st.shape[0]
    first = pl.multiple_of(ncp - i * (Q_BLOCK // CMP_STRIDE), Q_BLOCK // CMP_STRIDE)
    st = st + gd_ref[pl.ds(first, ncp), :]
    m = jnp.maximum(jnp.max(st, axis=0, keepdims=True), NEG_INF)
    p = jnp.exp2(st - m).astype(BF16)
    r = jnp.dot(lhs_ref[...], p, preferred_element_type=F32)
    l = r[HEAD_DIM:HEAD_DIM + 1]
    inv = jnp.where(l > 0.0, 1.0 / l, 0.0)
    oc_ref[...] = r[:HEAD_DIM] * inv
    imp4 = r[HEAD_DIM + ONES_ROWS:] * inv
    imp = imp4[:, 0:Q_BLOCK]
    for h in range(1, NSA_GQA):
        imp = imp + imp4[:, h * Q_BLOCK:(h + 1) * Q_BLOCK]

    shape = (LANES, Q_BLOCK)
    blk = lax.broadcasted_iota(jnp.int32, shape, 0).astype(F32)
    cur = ((t0 + lax.broadcasted_iota(jnp.int32, shape, 1)) // SEL_BLOCK).astype(F32)
    visible = blk <= cur
    forced = (blk == 0.0) | (blk == cur) | (blk == cur - 1.0)
    score = jnp.where(visible, imp + jnp.where(forced, FORCE_BONUS, 0.0), NEG_INF)
    score = jnp.where(blk < float(n_sel), score, REMOVED)

    def pick(_, carry):
        score, sel = carry
        mx = jnp.max(score, axis=0, keepdims=True)
        first = jnp.min(jnp.where(score == mx, blk, float(LANES)), axis=0, keepdims=True)
        hit = blk == first
        return jnp.where(hit, REMOVED, score), jnp.where(hit, 1.0, sel)

    _, sel = lax.fori_loop(0, top_n, pick, (score, jnp.zeros(shape, F32)))
    ns_ref[...] = jnp.where((sel > 0.0) & visible, 0.0, -MASK_BIG).T.astype(ns_ref.dtype)


def _cmp_attention(proj3, kv_cmp, gd, overlap, n_groups):
    b, s, _ = proj3.shape
    ncp = kv_cmp.shape[3]
    n_sel = s // SEL_BLOCK
    nq = s // Q_BLOCK
    width = NSA_GQA * HEAD_DIM
    vct = _with_ones_rows(kv_cmp[1].transpose(0, 1, 3, 2))
    lhs = jnp.concatenate([vct, jnp.broadcast_to(overlap.T[None, None], (b, n_groups, LANES, ncp))], axis=2)
    kern = functools.partial(_cmp_attn_kernel, n_sel=n_sel, top_n=min(SEL_TOPK, n_sel))
    return pl.pallas_call(
        kern,
        grid=(b, n_groups, nq),
        in_specs=[pl.BlockSpec((None, Q_BLOCK, width), lambda bi, g, i: (bi, i, g)),
                  pl.BlockSpec((None, None, None, ncp, HEAD_DIM), lambda bi, g, i: (0, bi, g, 0, 0)),
                  pl.BlockSpec((None, None, lhs.shape[2], ncp), lambda bi, g, i: (bi, g, 0, 0)),
                  pl.BlockSpec((None, 2 * ncp, width), lambda bi, g, i: (g, 0, 0))],
        out_specs=[pl.BlockSpec((None, None, None, HEAD_DIM, width), lambda bi, g, i: (bi, g, i, 0, 0)),
                   pl.BlockSpec((None, None, Q_BLOCK, LANES), lambda bi, g, i: (bi, g, i, 0))],
        out_shape=[jax.ShapeDtypeStruct((b, n_groups, nq, HEAD_DIM, width), F32),
                   jax.ShapeDtypeStruct((b, n_groups, s, LANES), BF16)],
        compiler_params=_cparams(("parallel", "parallel", "parallel")),
    )(proj3, kv_cmp, lhs, gd)


def _sel_win_kernel(q_ref, ka_ref, vst_ref, kw_ref, vwt_ref, ns_ref, oc_ref, g_ref, bn_ref, bw_ref,
                    o_ref, m_ref, acc_ref):
    i = pl.program_id(2)
    t0 = pl.multiple_of(i * Q_BLOCK, Q_BLOCK)
    q4 = _stack_heads(q_ref)

    ns = ns_ref[...]
    lane = lax.broadcasted_iota(jnp.int32, ns.shape, 1)
    near_blk = (lane >= 2 * i - 2) & (lane <= 2 * i + 1)
    ns_far = jnp.where(near_blk, -MASK_BIG, ns.astype(F32)).astype(BF16)
    qa_far = jnp.concatenate([q4, jnp.concatenate([ns_far] * NSA_GQA, axis=0)], axis=1)
    qa_near = jnp.concatenate([q4, jnp.concatenate([ns] * NSA_GQA, axis=0)], axis=1)
    _softmax_init(m_ref, acc_ref)

    blocks_per_tile = SEL_TK // Q_BLOCK

    def far_tile(j):
        start = pl.multiple_of(Q_BLOCK + j * SEL_TK, Q_BLOCK)
        st = lax.dot_general(ka_ref[pl.ds(start, SEL_TK), :], qa_far, _NT, preferred_element_type=F32)
        return st, _lane_concat(vst_ref[pl.ds(1 + j * blocks_per_tile, blocks_per_tile)], blocks_per_tile)

    def near_tile():
        st = lax.dot_general(ka_ref[pl.ds(t0, NEAR), :], qa_near, _NT, preferred_element_type=F32)
        st = st + bn_ref[...]
        key_pos = t0 - Q_BLOCK + lax.broadcasted_iota(jnp.int32, st.shape, 0)
        return jnp.where(key_pos >= 0, st, -jnp.inf), _lane_concat(vst_ref[pl.ds(i, 2)], 2)

    n_far = jnp.where(i == 0, 0, (i + 2) // 4)

    def far_quad(j, carry):
        _online_steps([far_tile(TILES_PER_TRIP * j + sub) for sub in range(TILES_PER_TRIP)], m_ref, acc_ref)
        return carry

    n_quads = n_far // TILES_PER_TRIP
    lax.fori_loop(0, n_quads, far_quad, 0)

    @pl.when(n_far - TILES_PER_TRIP * n_quads >= 2)
    def _():
        _online_steps([far_tile(TILES_PER_TRIP * n_quads), far_tile(TILES_PER_TRIP * n_quads + 1)], m_ref, acc_ref)

    @pl.when(n_far % 2 == 1)
    def _():
        _online_steps([far_tile(n_far - 1), near_tile()], m_ref, acc_ref)

    @pl.when(n_far % 2 == 0)
    def _():
        _online_steps([near_tile()], m_ref, acc_ref)

    o_sel = _normalized(acc_ref[...])

    span = Q_BLOCK + WINDOW
    sw = lax.dot_general(kw_ref[pl.ds(t0, span), :], q4, _NT, preferred_element_type=F32)
    sw = sw + bw_ref[...]
    key_pos = t0 - WINDOW + lax.broadcasted_iota(jnp.int32, sw.shape, 0)
    sw = jnp.where(key_pos >= 0, sw, -jnp.inf)
    pw = jnp.exp2(sw - jnp.max(sw, axis=0, keepdims=True)).astype(BF16)
    n_blk = span // Q_BLOCK
    o_win = _normalized(jnp.dot(_lane_concat(vwt_ref[pl.ds(i, n_blk)], n_blk), pw,
                                preferred_element_type=F32))

    gate_t = jax.nn.sigmoid(g_ref[...]).T
    for r in range(NSA_GQA):
        lo, hi = r * Q_BLOCK, (r + 1) * Q_BLOCK
        mix_t = (gate_t[3 * r:3 * r + 1, :] * oc_ref[:, lo:hi]
                 + gate_t[3 * r + 1:3 * r + 2, :] * o_sel[:, lo:hi]
                 + gate_t[3 * r + 2:3 * r + 3, :] * o_win[:, lo:hi])
        o_ref[:, lo:hi] = mix_t.T


def _sel_win_attention(proj3, k_aug, vt_sel, k_win, vt_win, negsel, o_cmp, gates, bias_near, bias_win, n_groups):
    b, s, _ = proj3.shape
    width = NSA_GQA * HEAD_DIM
    rows = NSA_GQA * Q_BLOCK
    sp_sel = k_aug.shape[1]
    sp_win = k_win.shape[1]
    nb_sel = vt_sel.shape[2]
    nb_win = vt_win.shape[2]
    return pl.pallas_call(
        _sel_win_kernel,
        grid=(b, n_groups, s // Q_BLOCK),
        in_specs=[pl.BlockSpec((None, Q_BLOCK, width), lambda bi, g, i: (bi, i, g)),
                  pl.BlockSpec((None, sp_sel, 2 * HEAD_DIM), lambda bi, g, i: (bi, 0, g)),
                  pl.BlockSpec((None, None, nb_sel, HEAD_DIM + ONES_ROWS, Q_BLOCK), lambda bi, g, i: (bi, g, 0, 0, 0)),
                  pl.BlockSpec((None, sp_win, HEAD_DIM), lambda bi, g, i: (bi, 0, g)),
                  pl.BlockSpec((None, None, nb_win, HEAD_DIM + ONES_ROWS, Q_BLOCK), lambda bi, g, i: (bi, g, 0, 0, 0)),
                  pl.BlockSpec((None, None, Q_BLOCK, LANES), lambda bi, g, i: (bi, g, i, 0)),
                  pl.BlockSpec((None, None, None, HEAD_DIM, width), lambda bi, g, i: (bi, g, i, 0, 0)),
                  pl.BlockSpec((None, Q_BLOCK, LANES), lambda bi, g, i: (bi, i, g)),
                  pl.BlockSpec((None, NEAR, rows), lambda bi, g, i: (g, 0, 0)),
                  pl.BlockSpec((None, Q_BLOCK + WINDOW, rows), lambda bi, g, i: (g, 0, 0))],
        out_specs=pl.BlockSpec((None, Q_BLOCK, width), lambda bi, g, i: (bi, i, g)),
        out_shape=jax.ShapeDtypeStruct((b, s, n_groups * width), F32),
        scratch_shapes=[pltpu.VMEM((1, rows), F32), pltpu.VMEM((HEAD_DIM + ONES_ROWS, rows), F32)],
        compiler_params=_cparams(("parallel", "parallel", "arbitrary")),
    )(proj3, k_aug, vt_sel, k_win, vt_win, negsel, o_cmp, gates, bias_near, bias_win)


def _router_kernel(h_ref, w_ref, o_ref, *, n_experts):
    logits = jnp.dot(h_ref[...], w_ref[...], preferred_element_type=F32)
    lane = lax.broadcasted_iota(jnp.int32, logits.shape, 1).astype(F32)
    logits = jnp.where(lane < float(n_experts), logits, -jnp.inf)
    m1 = jnp.max(logits, axis=1, keepdims=True)
    i1 = jnp.min(jnp.where(logits == m1, lane, float(LANES)), axis=1, keepdims=True)
    rest = jnp.where(lane == i1, -jnp.inf, logits)
    m2 = jnp.max(rest, axis=1, keepdims=True)
    i2 = jnp.min(jnp.where(rest == m2, lane, float(LANES)), axis=1, keepdims=True)
    e2 = jnp.exp(m2 - m1)
    denom = 1.0 + e2
    o_ref[...] = (jnp.where(lane == 0.0, i1, 0.0) + jnp.where(lane == 1.0, i2, 0.0)
                  + jnp.where(lane == 2.0, 1.0 / denom, 0.0) + jnp.where(lane == 3.0, e2 / denom, 0.0))


def _router(h, w_router_padded, n_experts):
    t, d = h.shape
    tm = _pick(t, (512, 256, 128))
    return pl.pallas_call(
        functools.partial(_router_kernel, n_experts=n_experts),
        grid=(t // tm,),
        in_specs=[pl.BlockSpec((tm, d), lambda i: (i, 0)), pl.BlockSpec((d, LANES), lambda i: (0, 0))],
        out_specs=pl.BlockSpec((tm, LANES), lambda i: (i, 0)),
        out_shape=jax.ShapeDtypeStruct((t, LANES), F32),
        compiler_params=_cparams(("parallel",)),
    )(h, w_router_padded)


def _row_copy(src_ref, dst_ref, sem, src_row, dst_row):
    return pltpu.make_async_copy(src_ref.at[pl.ds(src_row, 1)], dst_ref.at[pl.ds(dst_row, 1)], sem)


def _gather_rows_kernel(idx_ref, src_ref, out_ref, sem, *, rows):
    base = pl.program_id(0) * rows

    def start(r2, carry):
        for lane in range(2):
            r = 2 * r2 + lane
            _row_copy(src_ref, out_ref, sem, idx_ref[base + r], r).start(priority=lane)
        return carry

    lax.fori_loop(0, rows // 2, start, 0)

    def wait(r, carry):
        _row_copy(src_ref, out_ref, sem, 0, r).wait()
        return carry

    lax.fori_loop(0, rows, wait, 0)


def _gather_rows(src, idx):
    n_out = idx.shape[0]
    n_src, width = src.shape
    rows = _pick(n_out, (GATHER_ROWS, 256, 128))
    src = src.reshape(n_src, width // LANES, LANES)
    out = pl.pallas_call(
        functools.partial(_gather_rows_kernel, rows=rows),
        grid_spec=pltpu.PrefetchScalarGridSpec(
            num_scalar_prefetch=1,
            grid=(n_out // rows,),
            in_specs=[pl.BlockSpec(memory_space=pl.ANY)],
            out_specs=pl.BlockSpec((rows,) + src.shape[1:], lambda c, idx_ref: (c, 0, 0)),
            scratch_shapes=[pltpu.SemaphoreType.DMA(())]),
        out_shape=jax.ShapeDtypeStruct((n_out,) + src.shape[1:], src.dtype),
        compiler_params=_cparams(("arbitrary",)),
    )(idx, src)
    return out.reshape(n_out, width)


def _gmm_kernel(te_ref, nv_ref, x_ref, w_ref, *rest, swiglu):
    o_ref = rest[-1]
    live = pl.program_id(1) < nv_ref[0]

    @pl.when(live)
    def _():
        x = x_ref[...]
        r = jnp.dot(x, w_ref[...].astype(BF16), preferred_element_type=F32)
        if swiglu:
            r = (r * jax.nn.sigmoid(r)) * jnp.dot(x, rest[0][...].astype(BF16), preferred_element_type=F32)
        o_ref[...] = r.astype(o_ref.dtype)

    @pl.when(jnp.logical_not(live))
    def _():
        o_ref[...] = jnp.zeros_like(o_ref)


def _grouped_matmul(x, w, tile_expert, n_valid, *, w2=None, tn=512):
    p, kdim = x.shape
    n = w.shape[2]
    tn = _pick(n, (tn, 256, 128))
    wspec = pl.BlockSpec((None, kdim, tn), lambda j, i, te, nv: (te[i], 0, j))
    in_specs = [pl.BlockSpec((MOE_TM, kdim), lambda j, i, te, nv: (i, 0)), wspec]
    args = [x, w]
    if w2 is not None:
        in_specs.append(wspec)
        args.append(w2)
    return pl.pallas_call(
        functools.partial(_gmm_kernel, swiglu=w2 is not None),
        grid_spec=pltpu.PrefetchScalarGridSpec(
            num_scalar_prefetch=2,
            grid=(n // tn, p // MOE_TM),
            in_specs=in_specs,
            out_specs=pl.BlockSpec((MOE_TM, tn), lambda j, i, te, nv: (i, j))),
        out_shape=jax.ShapeDtypeStruct((p, n), BF16),
        compiler_params=_cparams(("parallel", "parallel")),
    )(tile_expert, n_valid, *args)


def _combine_kernel(x_ref, ya_ref, yb_ref, r_ref, o_ref):
    route = r_ref[...]
    o_ref[...] = (x_ref[...] + route[:, 2:3] * ya_ref[...].astype(F32)
                  + route[:, 3:4] * yb_ref[...].astype(F32))


def _moe_combine(x, y2, route):
    t, d = x.shape
    tm = _pick(t, (256, 128))
    nt = t // tm
    return pl.pallas_call(
        _combine_kernel,
        grid=(nt,),
        in_specs=[pl.BlockSpec((tm, d), lambda i: (i, 0)),
                  pl.BlockSpec((tm, d), lambda i: (i, 0)),
                  pl.BlockSpec((tm, d), lambda i: (i + nt, 0)),
                  pl.BlockSpec((tm, LANES), lambda i: (i, 0))],
        out_specs=pl.BlockSpec((tm, d), lambda i: (i, 0)),
        out_shape=jax.ShapeDtypeStruct((t, d), F32),
        compiler_params=_cparams(("parallel",)),
    )(x, y2, y2, route)


def _route_plan(i1, i2, n_experts):
    t = i1.shape[0]
    e_flat = jnp.concatenate([i1, i2])
    onehot = (e_flat[:, None] == jnp.arange(n_experts, dtype=jnp.int32)[None, :]).astype(jnp.int32)
    csum = jnp.cumsum(onehot, axis=0)
    counts = csum[-1]
    rank = jnp.sum((csum - onehot) * onehot, axis=1)
    padded = ((counts + MOE_TM - 1) // MOE_TM) * MOE_TM
    ends = jnp.cumsum(padded)
    pos = jnp.sum(onehot * (ends - padded)[None, :], axis=1) + rank
    p_rows = 2 * t + n_experts * MOE_TM
    token = jnp.tile(jnp.arange(t, dtype=jnp.int32), 2)
    src = jnp.zeros((p_rows,), jnp.int32).at[pos].set(token)
    tile_start = jnp.arange(p_rows // MOE_TM, dtype=jnp.int32) * MOE_TM
    tile_expert = jnp.minimum(jnp.sum((tile_start[:, None] >= ends[None, :]).astype(jnp.int32), axis=1),
                              n_experts - 1)
    n_valid = (ends[-1] // MOE_TM).reshape(1)
    return pos.astype(jnp.int32), src, tile_expert.astype(jnp.int32), n_valid.astype(jnp.int32)


def _moe_ffn(x, h, router_w, wg, wu, wd):
    n_experts = router_w.shape[1]
    route = _router(h, jnp.pad(router_w, ((0, 0), (0, LANES - n_experts))).astype(BF16), n_experts)
    i1 = route[:, 0].astype(jnp.int32)
    i2 = route[:, 1].astype(jnp.int32)
    pos, src, tile_expert, n_valid = _route_plan(i1, i2, n_experts)
    xs = _gather_rows(h, src)
    act = _grouped_matmul(xs, wg, tile_expert, n_valid, w2=wu)
    y = _grouped_matmul(act, wd, tile_expert, n_valid)
    return _moe_combine(x, _gather_rows(y, pos), route)


def _static_tables(s):
    n_sel = s // SEL_BLOCK
    ncp = max(LANES, -(-(s // CMP_STRIDE) // LANES) * LANES)
    n_cmp = (s - CMP_BLOCK) // CMP_STRIDE + 1
    span = Q_BLOCK + WINDOW
    d_near = (NEAR - 1) - np.arange(NEAR + Q_BLOCK - 1)
    d_win = (span - 1) - np.arange(span + Q_BLOCK - 1)
    r = np.arange(Q_BLOCK)[:, None]
    d_cmp = r - CMP_STRIDE * np.arange(CMP_NEAR[0], CMP_NEAR[1])[None, :] - (CMP_BLOCK - 1)
    c0 = np.arange(ncp)[:, None] * CMP_STRIDE
    j0 = np.arange(LANES)[None, :] * SEL_BLOCK
    overlap = np.clip(np.minimum(c0 + CMP_BLOCK, j0 + SEL_BLOCK) - np.maximum(c0, j0), 0, None).astype(np.float32) / CMP_BLOCK
    overlap[n_cmp:, :] = 0.0
    overlap[:, n_sel:] = 0.0
    onehot = (np.arange(s)[:, None] // SEL_BLOCK == np.arange(LANES)[None, :]).astype(np.float32)
    return dict(
        ncp=ncp,
        b_near=_bucket_np(d_near), ok_near=d_near >= 0,
        b_win=_bucket_np(d_win), ok_win=(d_win >= 0) & (d_win < WINDOW),
        b_cmp=_bucket_np(d_cmp), ok_cmp=(d_cmp >= 0) & (d_cmp < FAR_DIST), future_cmp=d_cmp < 0,
        overlap=overlap, onehot=onehot)


def _bias_tables(rel_bias, tabs, n_groups):
    tbl = rel_bias.T.astype(F32) * LOG2E
    reb = tbl - tbl[:, FAR_BUCKET:FAR_BUCKET + 1]
    heads = reb.shape[0]

    def toeplitz(bucket, ok, width):
        seq = jnp.where(jnp.asarray(ok)[None, :], reb[:, jnp.asarray(bucket)], -MASK_BIG)
        period = width + Q_BLOCK
        seq = jnp.pad(seq, ((0, 0), (0, period - seq.shape[1])))
        rolled = jnp.tile(seq, (1, Q_BLOCK + 1))[:, :Q_BLOCK * (period + 1)].reshape(heads, Q_BLOCK, period + 1)
        tile = rolled[:, ::-1, :width]
        tile = tile.reshape(n_groups, NSA_GQA, Q_BLOCK, width).transpose(0, 3, 1, 2)
        return tile.reshape(n_groups, width, NSA_GQA * Q_BLOCK)

    ncp = tabs["ncp"]
    win = jnp.where(jnp.asarray(tabs["ok_cmp"])[None], reb[:, jnp.asarray(tabs["b_cmp"])], 0.0)
    win = jnp.where(jnp.asarray(tabs["future_cmp"])[None], -MASK_BIG, win).transpose(0, 2, 1)
    lo = ncp + CMP_NEAR[0]
    past = jnp.zeros((heads, lo, Q_BLOCK), F32)
    future = jnp.full((heads, 2 * ncp - lo - win.shape[1], Q_BLOCK), -MASK_BIG, F32)
    cmp_tab = jnp.concatenate([past, win, future], axis=1)
    cmp_tab = cmp_tab.reshape(n_groups, NSA_GQA, 2 * ncp, Q_BLOCK).transpose(0, 2, 1, 3)
    return (toeplitz(tabs["b_near"], tabs["ok_near"], NEAR),
            toeplitz(tabs["b_win"], tabs["ok_win"], Q_BLOCK + WINDOW),
            cmp_tab.reshape(n_groups, 2 * ncp, NSA_GQA * Q_BLOCK))


def _key_blocks_t(a, g, front_blocks):
    b, s, _ = a.shape
    a = _with_ones_rows(a.reshape(b, s // Q_BLOCK, Q_BLOCK, g, HEAD_DIM).transpose(0, 3, 1, 4, 2))
    return jnp.pad(a, ((0, 0), (0, 0), (front_blocks, 0), (0, 0), (0, 0)))


def _mixer(h, w_in, f_bias, cmp_pe, cmp_w1, cmp_w2, biases, tabs, b, s):
    d = h.shape[1]
    n_heads = d // HEAD_DIM
    hn = n_heads // 2
    hf = n_heads - hn
    g = hn // NSA_GQA
    gw = g * HEAD_DIM
    qn_w, fox_w = hn * HEAD_DIM, hf * HEAD_DIM
    splits = (qn_w, gw, gw, gw, gw, gw, gw, hn * 3, fox_w, fox_w, fox_w, hf)
    offs = np.concatenate([[0], np.cumsum(splits)])
    col = lambda k: w_in[:, offs[k]:offs[k + 1]]
    scale = HEAD_DIM ** -0.5 * LOG2E

    w_main = jnp.concatenate([col(k) for k in (0, 1, 2, 3, 4, 5, 6, 8, 9, 10)], axis=1).astype(BF16)
    blocks = np.cumsum([0, hn, g, g, g, g, g, g, hf, hf, hf])
    cscale = np.ones((1, w_main.shape[1]), np.float32)
    cscale[0, :qn_w] = scale
    cscale[0, blocks[7] * HEAD_DIM:blocks[8] * HEAD_DIM] = scale
    proj = _matmul(h, w_main, cscale=jnp.asarray(cscale), out_dtype=BF16)
    proj3 = proj.reshape(b, s, -1)
    grp = lambda k: proj3[:, :, blocks[k] * HEAD_DIM:blocks[k + 1] * HEAD_DIM]

    w_gate = jnp.pad(col(7).reshape(d, g, NSA_GQA * 3), ((0, 0), (0, 0), (0, LANES - NSA_GQA * 3))).reshape(d, g * LANES)
    w_f = jnp.pad(col(11), ((0, 0), (0, LANES - hf)))
    small = _matmul(h, jnp.concatenate([w_gate, w_f], axis=1).astype(BF16), tn=(g + 1) * LANES)
    small3 = small.reshape(b, s, -1)
    gates = small3[:, :, :g * LANES]
    f_raw = small3[:, :, g * LANES:]

    cum = _forget_cumsum(f_raw, jnp.pad(f_bias.astype(F32), (0, LANES - hf)).reshape(1, LANES))
    o_fox = _fox_attention(proj3, int(blocks[7]), int(blocks[8]), grp(9).reshape(b, s, hf, HEAD_DIM),
                           cum[:, :, :hf])

    ncp = tabs["ncp"]

    def to_rows(a):
        a = a.reshape(b, s // CMP_STRIDE, CMP_STRIDE, g, HEAD_DIM).transpose(0, 3, 1, 2, 4)
        a = a.reshape(b, g, s // CMP_STRIDE, CMP_STRIDE * HEAD_DIM)
        return jnp.pad(a, ((0, 0), (0, 0), (0, ncp - s // CMP_STRIDE), (0, 0)))

    rows = jnp.stack([to_rows(grp(1)), to_rows(grp(2))])
    half = CMP_STRIDE * HEAD_DIM
    w1cat = jnp.concatenate([cmp_w1[:, :half], cmp_w1[:, half:]], axis=2).astype(BF16)
    pe_flat = jnp.pad(cmp_pe.reshape(2, 1, CMP_BLOCK * HEAD_DIM), ((0, 0), (0, 7), (0, 0))).astype(BF16)
    kv_cmp = _compress(rows, w1cat, pe_flat, cmp_w2.astype(BF16))

    bias_near, bias_win, bias_cmp = biases
    o_cmp, negsel = _cmp_attention(proj3, kv_cmp, bias_cmp, jnp.asarray(tabs["overlap"], BF16), g)

    onehot = jnp.broadcast_to(jnp.asarray(tabs["onehot"], BF16)[None, :, None, :], (b, s, g, LANES))
    k_aug = jnp.concatenate([grp(3).reshape(b, s, g, HEAD_DIM), onehot], axis=-1).reshape(b, s, g * 2 * HEAD_DIM)
    front = lambda a, n: jnp.pad(a, ((0, 0), (n, 0), (0, 0)))
    o_nsa = _sel_win_attention(proj3, front(k_aug, Q_BLOCK), _key_blocks_t(grp(4), g, 1),
                               front(grp(5), WINDOW), _key_blocks_t(grp(6), g, WINDOW // Q_BLOCK),
                               negsel, o_cmp, gates, bias_near, bias_win, g)
    return o_nsa.reshape(b * s, qn_w), o_fox.reshape(b * s, fox_w)


def _pad_to(a, axis, mult):
    n = a.shape[axis]
    target = -(-n // mult) * mult
    if target == n:
        return a
    pad = [(0, 0)] * a.ndim
    pad[axis] = (0, target - n)
    return jnp.pad(a, pad)


def _dense_ffn(x, h, wg, wu, wd):
    ff_tile = 512
    wg = _pad_to(wg, 1, ff_tile).astype(BF16)
    wu = _pad_to(wu, 1, ff_tile).astype(BF16)
    wd = _pad_to(wd, 0, ff_tile).astype(BF16)
    act = _matmul(h, wg, w2=wu, out_dtype=BF16, tn=ff_tile)
    return _matmul(act, wd, res=x, tm=1024, tn=1024, tk=wd.shape[0] // 4)


def kernel(x, attn_norm, w_in, fgate_bias, cmp_pe, cmp_w1, cmp_w2, rel_bias, out_norm_nsa, out_norm_fox,
           w_out, ffn_norm, dense_w_gate, dense_w_up, dense_w_down, router_w, moe_w_gate, moe_w_up,
           moe_w_down, final_norm):
    b, s, d = x.shape
    depth = attn_norm.shape[0]
    n_groups = (d // HEAD_DIM // 2) // NSA_GQA
    assert s % SEL_TK == 0 and s // SEL_BLOCK <= LANES and d % (2 * NSA_GQA * HEAD_DIM) == 0
    assert (b * s) % MOE_TM == 0
    tabs = _static_tables(s)
    biases = _bias_tables(rel_bias, tabs, n_groups)
    xt = x.reshape(b * s, d)
    for layer in range(depth):
        h = _rmsnorm(xt, attn_norm[layer], BF16)
        o_nsa, o_fox = _mixer(h, w_in[layer], fgate_bias[layer], cmp_pe[layer], cmp_w1[layer], cmp_w2[layer],
                              biases, tabs, b, s)
        mixed = _pair_rmsnorm(o_nsa, o_fox, out_norm_nsa[layer], out_norm_fox[layer])
        xt = _matmul(mixed, w_out[layer].astype(BF16), res=xt)
        h = _rmsnorm(xt, ffn_norm[layer], BF16)
        i = layer // 2
        if layer % 2 == 0:
            xt = _dense_ffn(xt, h, dense_w_gate[i], dense_w_up[i], dense_w_down[i])
        else:
            xt = _moe_ffn(xt, h, router_w[i], moe_w_gate[i], moe_w_up[i], moe_w_down[i])
    return _rmsnorm(xt, final_norm, x.dtype).reshape(b, s, d)
```

```python
import functools
import math

import numpy as np
import jax
import jax.numpy as jnp
from jax import lax
from jax.experimental import pallas as pl
from jax.experimental.pallas import tpu as pltpu

F32 = jnp.float32
BF16 = jnp.bfloat16

HEAD_DIM = 128
NSA_GQA = 4
CMP_BLOCK = 32
CMP_STRIDE = 16
SEL_BLOCK = 64
SEL_TOPK = 16
WINDOW = 512
Q_BLOCK = 128
N_BUCKETS = 32
MAX_DISTANCE = 128
TOP_K = 2
EPS = 1e-6
NEG_INF = -1e30
FORCE_BONUS = 1e4

LANES = 128
VMEM_LIMIT = 56 * 1024 * 1024

LOG2E = 1.4426950408889634
MASK_BIG = 2.0 ** 101
REMOVED = -3.0e38
FAR_BUCKET = N_BUCKETS - 1
FAR_DIST = 113
SEL_TK = 512
NEAR = 2 * Q_BLOCK
CMP_NEAR = (-9, 7)
TILES_PER_TRIP = 4
ONES_ROWS = 16
MOE_TM = 512
GATHER_ROWS = 512

_NT = (((1,), (1,)), ((), ()))


def _cparams(sem):
    return pltpu.CompilerParams(dimension_semantics=sem, vmem_limit_bytes=VMEM_LIMIT)


def _bucket_np(d):
    max_exact = N_BUCKETS // 2
    dd = np.maximum(d, 0)
    ratio = np.log(np.maximum(dd, max_exact).astype(np.float32) / max_exact) / math.log(MAX_DISTANCE / max_exact)
    large = np.minimum(max_exact + (ratio * (N_BUCKETS - max_exact)).astype(np.int32), N_BUCKETS - 1)
    return np.where(dd < max_exact, dd, large).astype(np.int32)


def _pick(n, prefs):
    for p in prefs:
        if n % p == 0:
            return p
    return n


def _rmsnorm_kernel(x_ref, g_ref, o_ref):
    x = x_ref[...].astype(F32)
    y = x * lax.rsqrt(jnp.mean(x * x, axis=-1, keepdims=True) + EPS)
    o_ref[...] = (y * g_ref[...]).astype(o_ref.dtype)


def _rmsnorm(x, gain, out_dtype):
    t, d = x.shape
    tm = _pick(t, (512, 256, 128))
    return pl.pallas_call(
        _rmsnorm_kernel,
        grid=(t // tm,),
        in_specs=[pl.BlockSpec((tm, d), lambda i: (i, 0)), pl.BlockSpec((1, d), lambda i: (0, 0))],
        out_specs=pl.BlockSpec((tm, d), lambda i: (i, 0)),
        out_shape=jax.ShapeDtypeStruct((t, d), out_dtype),
        compiler_params=_cparams(("parallel",)),
    )(x, gain.reshape(1, d).astype(F32))


def _pair_rmsnorm_kernel(a_ref, b_ref, ga_ref, gb_ref, o_ref):
    wa = a_ref.shape[-1]
    for ref, g, lo in ((a_ref, ga_ref, 0), (b_ref, gb_ref, wa)):
        x = ref[...].astype(F32)
        y = x * lax.rsqrt(jnp.mean(x * x, axis=-1, keepdims=True) + EPS)
        o_ref[:, lo:lo + x.shape[-1]] = (y * g[...]).astype(o_ref.dtype)


def _pair_rmsnorm(a, b, ga, gb):
    t, wa = a.shape
    wb = b.shape[1]
    tm = _pick(t, (512, 256, 128))
    return pl.pallas_call(
        _pair_rmsnorm_kernel,
        grid=(t // tm,),
        in_specs=[pl.BlockSpec((tm, wa), lambda i: (i, 0)), pl.BlockSpec((tm, wb), lambda i: (i, 0)),
                  pl.BlockSpec((1, wa), lambda i: (0, 0)), pl.BlockSpec((1, wb), lambda i: (0, 0))],
        out_specs=pl.BlockSpec((tm, wa + wb), lambda i: (i, 0)),
        out_shape=jax.ShapeDtypeStruct((t, wa + wb), BF16),
        compiler_params=_cparams(("parallel",)),
    )(a, b, ga.reshape(1, wa).astype(F32), gb.reshape(1, wb).astype(F32))


def _mm_kernel(*refs, nk, swiglu, has_cscale, has_res):
    it = iter(refs)
    x_ref = next(it)
    w_ref = next(it)
    w2_ref = next(it) if swiglu else None
    cs_ref = next(it) if has_cscale else None
    res_ref = next(it) if has_res else None
    o_ref = next(it)
    acc_ref = next(it) if nk > 1 else None
    acc2_ref = next(it) if (nk > 1 and swiglu) else None

    x = x_ref[...]

    def epilogue(r, r2):
        if swiglu:
            r = (r * jax.nn.sigmoid(r)) * r2
        if has_cscale:
            r = r * cs_ref[...]
        if has_res:
            r = res_ref[...] + r
        o_ref[...] = r.astype(o_ref.dtype)

    if nk == 1:
        r = jnp.dot(x, w_ref[...], preferred_element_type=F32)
        r2 = jnp.dot(x, w2_ref[...], preferred_element_type=F32) if swiglu else None
        epilogue(r, r2)
        return

    k = pl.program_id(2)

    def accum(acc, w):
        part = jnp.dot(x, w[...], preferred_element_type=F32)

        @pl.when(k == 0)
        def _():
            acc[...] = part

        @pl.when(k > 0)
        def _():
            acc[...] += part

    accum(acc_ref, w_ref)
    if swiglu:
        accum(acc2_ref, w2_ref)

    @pl.when(k == nk - 1)
    def _():
        epilogue(acc_ref[...], acc2_ref[...] if swiglu else None)


def _matmul(x, w, *, w2=None, cscale=None, res=None, out_dtype=F32, tm=1024, tn=512, tk=None):
    m, kdim = x.shape
    n = w.shape[1]
    tm = _pick(m, (tm, 512, 256, 128))
    tn = _pick(n, (tn, 512, 256, 128))
    tk = kdim if tk is None else tk
    assert kdim % tk == 0
    nk = kdim // tk
    swiglu = w2 is not None
    in_specs = [pl.BlockSpec((tm, tk), lambda i, j, k: (i, k)),
                pl.BlockSpec((tk, tn), lambda i, j, k: (k, j))]
    args = [x, w]
    if swiglu:
        in_specs.append(pl.BlockSpec((tk, tn), lambda i, j, k: (k, j)))
        args.append(w2)
    if cscale is not None:
        in_specs.append(pl.BlockSpec((1, tn), lambda i, j, k: (0, j)))
        args.append(cscale)
    if res is not None:
        in_specs.append(pl.BlockSpec((tm, tn), lambda i, j, k: (i, j)))
        args.append(res)
    scratch = []
    if nk > 1:
        scratch = [pltpu.VMEM((tm, tn), F32)] * (2 if swiglu else 1)
    kern = functools.partial(_mm_kernel, nk=nk, swiglu=swiglu, has_cscale=cscale is not None,
                             has_res=res is not None)
    return pl.pallas_call(
        kern,
        grid=(m // tm, n // tn, nk),
        in_specs=in_specs,
        out_specs=pl.BlockSpec((tm, tn), lambda i, j, k: (i, j)),
        out_shape=jax.ShapeDtypeStruct((m, n), out_dtype),
        scratch_shapes=scratch,
        compiler_params=_cparams(("parallel", "parallel", "arbitrary")),
    )(*args)


def _cumsum_kernel(f_ref, b_ref, o_ref, carry_ref):
    j = pl.program_id(1)

    @pl.when(j == 0)
    def _():
        carry_ref[...] = jnp.zeros_like(carry_ref)

    z = f_ref[...] + b_ref[...]
    logf = jnp.minimum(z, 0.0) - jnp.log1p(jnp.exp(-jnp.abs(z)))
    ts = z.shape[0]
    row = lax.broadcasted_iota(jnp.int32, (ts, ts), 0)
    col = lax.broadcasted_iota(jnp.int32, (ts, ts), 1)
    tri = jnp.where(col <= row, 1.0, 0.0).astype(F32)
    cum = jnp.dot(tri, logf, preferred_element_type=F32, precision=lax.Precision.HIGHEST)
    cum = cum + carry_ref[0:1, :]
    carry_ref[...] = jnp.broadcast_to(cum[ts - 1:ts, :], carry_ref.shape)
    o_ref[...] = cum


def _forget_cumsum(f_raw, f_bias):
    b, s, _ = f_raw.shape
    ts = _pick(s, (256, 128))
    return pl.pallas_call(
        _cumsum_kernel,
        grid=(b, s // ts),
        in_specs=[pl.BlockSpec((None, ts, LANES), lambda bi, j: (bi, j, 0)),
                  pl.BlockSpec((1, LANES), lambda bi, j: (0, 0))],
        out_specs=pl.BlockSpec((None, ts, LANES), lambda bi, j: (bi, j, 0)),
        out_shape=jax.ShapeDtypeStruct((b, s, LANES), F32),
        scratch_shapes=[pltpu.VMEM((8, LANES), F32)],
        compiler_params=_cparams(("parallel", "arbitrary")),
    )(f_raw, f_bias)


def _online_steps(tiles, m_ref, acc_ref, query_bias=None):
    m = m_ref[...]
    acc = acc_ref[...]
    for st, vt in tiles:
        peak = jnp.max(st, axis=0, keepdims=True)
        if query_bias is not None:
            peak = peak + query_bias
        m_new = jnp.maximum(m, peak)
        alpha = jnp.exp2(m - m_new)
        shift = m_new if query_bias is None else m_new - query_bias
        p = jnp.exp2(st - shift).astype(BF16)
        acc = alpha * acc + jnp.dot(vt, p, preferred_element_type=F32)
        m = m_new
    m_ref[...] = m
    acc_ref[...] = acc


def _sweep_even(n_tiles, step):
    def trip(j, carry):
        step(TILES_PER_TRIP * j, TILES_PER_TRIP)
        return carry

    n_trips = n_tiles // TILES_PER_TRIP
    lax.fori_loop(0, n_trips, trip, 0)
    done = TILES_PER_TRIP * n_trips
    size = TILES_PER_TRIP // 2
    while size >= 2:
        take = ((n_tiles - done) & size) != 0

        @pl.when(take)
        def _(done=done, size=size):
            step(done, size)

        done = done + jnp.where(take, size, 0)
        size //= 2


def _softmax_init(m_ref, acc_ref):
    m_ref[...] = jnp.full_like(m_ref, NEG_INF)
    acc_ref[...] = jnp.zeros_like(acc_ref)


def _normalized(acc):
    return acc[:HEAD_DIM] / acc[HEAD_DIM:HEAD_DIM + 1]


def _with_ones_rows(vt):
    ones = jnp.ones(vt.shape[:-2] + (ONES_ROWS, vt.shape[-1]), vt.dtype)
    return jnp.concatenate([vt, ones], axis=-2)


def _lane_concat(blocks, n):
    return blocks[0] if n == 1 else jnp.concatenate([blocks[j] for j in range(n)], axis=1)


def _fox_kernel(q_ref, k_ref, vt_ref, c_ref, o_ref, m_ref, acc_ref, ck_ref, *, tq):
    qi = pl.program_id(2)
    nq = c_ref.shape[0]

    @pl.when(qi == 0)
    def _():
        for j in range(nq):
            ck_ref[j * tq:(j + 1) * tq, :] = jnp.broadcast_to(c_ref[j:j + 1, :], (LANES, tq)).T

    q = q_ref[...]
    cq = c_ref[pl.ds(qi, 1), :]
    _softmax_init(m_ref, acc_ref)

    def step(ki, diagonals):
        tiles = []
        for sub, diagonal in enumerate(diagonals):
            start = pl.multiple_of((ki + sub) * tq, tq)
            st = lax.dot_general(k_ref[pl.ds(start, tq), :], q, _NT, preferred_element_type=F32)
            ck = ck_ref[pl.ds(start, tq), :]
            st = st - jnp.concatenate([ck] * (tq // LANES), axis=1)
            if diagonal:
                key = lax.broadcasted_iota(jnp.int32, st.shape, 0)
                qry = lax.broadcasted_iota(jnp.int32, st.shape, 1)
                st = jnp.where(key <= qry, st, -jnp.inf)
            tiles.append((st, vt_ref[ki + sub]))
        _online_steps(tiles, m_ref, acc_ref, query_bias=cq)

    _sweep_even(qi, lambda first, count: step(first, (False,) * count))

    @pl.when(qi % 2 == 1)
    def _():
        step(qi - 1, (False, True))

    @pl.when(qi % 2 == 0)
    def _():
        step(qi, (True,))

    o_ref[...] = _normalized(acc_ref[...]).T


def _fox_attention(proj3, q0, k0, v, cum):
    b, s, h, hd = v.shape
    tq = _pick(s, (512, 256, 128))
    nq = s // tq
    c2 = (cum * LOG2E).transpose(0, 2, 1).reshape(b, h, nq, tq)
    vt = _with_ones_rows(v.reshape(b, nq, tq, h, hd).transpose(0, 3, 1, 4, 2))
    hv = hd + ONES_ROWS
    return pl.pallas_call(
        functools.partial(_fox_kernel, tq=tq),
        grid=(b, h, nq),
        in_specs=[pl.BlockSpec((None, tq, hd), lambda bi, hi, qi: (bi, qi, q0 + hi)),
                  pl.BlockSpec((None, s, hd), lambda bi, hi, qi: (bi, 0, k0 + hi)),
                  pl.BlockSpec((None, None, nq, hv, tq), lambda bi, hi, qi: (bi, hi, 0, 0, 0)),
                  pl.BlockSpec((None, None, nq, tq), lambda bi, hi, qi: (bi, hi, 0, 0))],
        out_specs=pl.BlockSpec((None, tq, hd), lambda bi, hi, qi: (bi, qi, hi)),
        out_shape=jax.ShapeDtypeStruct((b, s, h * hd), F32),
        scratch_shapes=[pltpu.VMEM((1, tq), F32), pltpu.VMEM((hv, tq), F32), pltpu.VMEM((s, LANES), F32)],
        compiler_params=_cparams(("parallel", "parallel", "arbitrary")),
    )(proj3, proj3, vt, c2)


def _compress_kernel(r_ref, w1_ref, pe_ref, w2_ref, o_ref):
    ab = jnp.dot(r_ref[...], w1_ref[...], preferred_element_type=F32)
    nc = ab.shape[0]
    half = pe_ref.shape[1] // 2
    first = ab[:, :HEAD_DIM]
    second = pltpu.roll(ab[:, HEAD_DIM:], nc - 1, axis=0)
    pe_term = (jnp.dot(pe_ref[:, :half], w1_ref[:, :HEAD_DIM], preferred_element_type=F32)
               + jnp.dot(pe_ref[:, half:], w1_ref[:, HEAD_DIM:], preferred_element_type=F32))
    pre = first + second + pe_term[0:1, :]
    hid = pre * jax.nn.sigmoid(pre)
    o_ref[...] = jnp.dot(hid.astype(BF16), w2_ref[...], preferred_element_type=F32).astype(o_ref.dtype)


def _compress(rows, w1cat, pe_flat, w2):
    two, b, g, nc, width = rows.shape
    return pl.pallas_call(
        _compress_kernel,
        grid=(two, b, g),
        in_specs=[pl.BlockSpec((None, None, None, nc, width), lambda a, bi, gi: (a, bi, gi, 0, 0)),
                  pl.BlockSpec((None, width, 2 * HEAD_DIM), lambda a, bi, gi: (a, 0, 0)),
                  pl.BlockSpec((None, 8, 2 * width), lambda a, bi, gi: (a, 0, 0)),
                  pl.BlockSpec((None, HEAD_DIM, HEAD_DIM), lambda a, bi, gi: (a, 0, 0))],
        out_specs=pl.BlockSpec((None, None, None, nc, HEAD_DIM), lambda a, bi, gi: (a, bi, gi, 0, 0)),
        out_shape=jax.ShapeDtypeStruct((two, b, g, nc, HEAD_DIM), BF16),
        compiler_params=_cparams(("parallel", "parallel", "parallel")),
    )(rows, w1cat, pe_flat, w2)


def _stack_heads(q_ref):
    return jnp.concatenate([q_ref[:, r * HEAD_DIM:(r + 1) * HEAD_DIM] for r in range(NSA_GQA)], axis=0)


def _cmp_attn_kernel(q_ref, kc_ref, lhs_ref, gd_ref, oc_ref, ns_ref, *, n_sel, top_n):
    i = pl.program_id(2)
    t0 = i * Q_BLOCK
    q4 = _stack_heads(q_ref)
    st = lax.dot_general(kc_ref[...], q4, _NT, preferred_element_type=F32)
    ncp = st.shape[0]
    first = pl.multiple_of(ncp - i * (Q_BLOCK // CMP_STRIDE), Q_BLOCK // CMP_STRIDE)
    st = st + gd_ref[pl.ds(first, ncp), :]
    m = jnp.maximum(jnp.max(st, axis=0, keepdims=True), NEG_INF)
    p = jnp.exp2(st - m).astype(BF16)
    r = jnp.dot(lhs_ref[...], p, preferred_element_type=F32)
    l = r[HEAD_DIM:HEAD_DIM + 1]
    inv = jnp.where(l > 0.0, 1.0 / l, 0.0)
    oc_ref[...] = r[:HEAD_DIM] * inv
    imp4 = r[HEAD_DIM + ONES_ROWS:] * inv
    imp = imp4[:, 0:Q_BLOCK]
    for h in range(1, NSA_GQA):
        imp = imp + imp4[:, h * Q_BLOCK:(h + 1) * Q_BLOCK]

    shape = (LANES, Q_BLOCK)
    blk = lax.broadcasted_iota(jnp.int32, shape, 0).astype(F32)
    cur = ((t0 + lax.broadcasted_iota(jnp.int32, shape, 1)) // SEL_BLOCK).astype(F32)
    visible = blk <= cur
    forced = (blk == 0.0) | (blk == cur) | (blk == cur - 1.0)
    score = jnp.where(visible, imp + jnp.where(forced, FORCE_BONUS, 0.0), NEG_INF)
    score = jnp.where(blk < float(n_sel), score, REMOVED)

    def pick(_, carry):
        score, sel = carry
        mx = jnp.max(score, axis=0, keepdims=True)
        first = jnp.min(jnp.where(score == mx, blk, float(LANES)), axis=0, keepdims=True)
        hit = blk == first
        return jnp.where(hit, REMOVED, score), jnp.where(hit, 1.0, sel)

    _, sel = lax.fori_loop(0, top_n, pick, (score, jnp.zeros(shape, F32)))
    ns_ref[...] = jnp.where((sel > 0.0) & visible, 0.0, -MASK_BIG).T.astype(ns_ref.dtype)


def _cmp_attention(proj3, kv_cmp, gd, overlap, n_groups):
    b, s, _ = proj3.shape
    ncp = kv_cmp.shape[3]
    n_sel = s // SEL_BLOCK
    nq = s // Q_BLOCK
    width = NSA_GQA * HEAD_DIM
    vct = _with_ones_rows(kv_cmp[1].transpose(0, 1, 3, 2))
    lhs = jnp.concatenate([vct, jnp.broadcast_to(overlap.T[None, None], (b, n_groups, LANES, ncp))], axis=2)
    kern = functools.partial(_cmp_attn_kernel, n_sel=n_sel, top_n=min(SEL_TOPK, n_sel))
    return pl.pallas_call(
        kern,
        grid=(b, n_groups, nq),
        in_specs=[pl.BlockSpec((None, Q_BLOCK, width), lambda bi, g, i: (bi, i, g)),
                  pl.BlockSpec((None, None, None, ncp, HEAD_DIM), lambda bi, g, i: (0, bi, g, 0, 0)),
                  pl.BlockSpec((None, None, lhs.shape[2], ncp), lambda bi, g, i: (bi, g, 0, 0)),
                  pl.BlockSpec((None, 2 * ncp, width), lambda bi, g, i: (g, 0, 0))],
        out_specs=[pl.BlockSpec((None, None, None, HEAD_DIM, width), lambda bi, g, i: (bi, g, i, 0, 0)),
                   pl.BlockSpec((None, None, Q_BLOCK, LANES), lambda bi, g, i: (bi, g, i, 0))],
        out_shape=[jax.ShapeDtypeStruct((b, n_groups, nq, HEAD_DIM, width), F32),
                   jax.ShapeDtypeStruct((b, n_groups, s, LANES), BF16)],
        compiler_params=_cparams(("parallel", "parallel", "parallel")),
    )(proj3, kv_cmp, lhs, gd)


def _sel_win_kernel(q_ref, ka_ref, vst_ref, kw_ref, vwt_ref, ns_ref, oc_ref, g_ref, bn_ref, bw_ref,
                    o_ref, m_ref, acc_ref):
    i = pl.program_id(2)
    t0 = pl.multiple_of(i * Q_BLOCK, Q_BLOCK)
    q4 = _stack_heads(q_ref)

    ns = ns_ref[...]
    lane = lax.broadcasted_iota(jnp.int32, ns.shape, 1)
    near_blk = (lane >= 2 * i - 2) & (lane <= 2 * i + 1)
    ns_far = jnp.where(near_blk, -MASK_BIG, ns.astype(F32)).astype(BF16)
    qa_far = jnp.concatenate([q4, jnp.concatenate([ns_far] * NSA_GQA, axis=0)], axis=1)
    qa_near = jnp.concatenate([q4, jnp.concatenate([ns] * NSA_GQA, axis=0)], axis=1)
    _softmax_init(m_ref, acc_ref)

    blocks_per_tile = SEL_TK // Q_BLOCK

    def far_tile(j):
        start = pl.multiple_of(Q_BLOCK + j * SEL_TK, Q_BLOCK)
        st = lax.dot_general(ka_ref[pl.ds(start, SEL_TK), :], qa_far, _NT, preferred_element_type=F32)
        return st, _lane_concat(vst_ref[pl.ds(1 + j * blocks_per_tile, blocks_per_tile)], blocks_per_tile)

    def near_tile():
        st = lax.dot_general(ka_ref[pl.ds(t0, NEAR), :], qa_near, _NT, preferred_element_type=F32)
        st = st + bn_ref[...]
        key_pos = t0 - Q_BLOCK + lax.broadcasted_iota(jnp.int32, st.shape, 0)
        return jnp.where(key_pos >= 0, st, -jnp.inf), _lane_concat(vst_ref[pl.ds(i, 2)], 2)

    n_far = jnp.where(i == 0, 0, (i + 2) // 4)

    _sweep_even(n_far, lambda first, count: _online_steps([far_tile(first + sub) for sub in range(count)],
                                                          m_ref, acc_ref))

    @pl.when(n_far % 2 == 1)
    def _():
        _online_steps([far_tile(n_far - 1), near_tile()], m_ref, acc_ref)

    @pl.when(n_far % 2 == 0)
    def _():
        _online_steps([near_tile()], m_ref, acc_ref)

    o_sel = _normalized(acc_ref[...])

    span = Q_BLOCK + WINDOW
    sw = lax.dot_general(kw_ref[pl.ds(t0, span), :], q4, _NT, preferred_element_type=F32)
    sw = sw + bw_ref[...]
    key_pos = t0 - WINDOW + lax.broadcasted_iota(jnp.int32, sw.shape, 0)
    sw = jnp.where(key_pos >= 0, sw, -jnp.inf)
    pw = jnp.exp2(sw - jnp.max(sw, axis=0, keepdims=True)).astype(BF16)
    n_blk = span // Q_BLOCK
    o_win = _normalized(jnp.dot(_lane_concat(vwt_ref[pl.ds(i, n_blk)], n_blk), pw,
                                preferred_element_type=F32))

    gate_t = jax.nn.sigmoid(g_ref[...]).T
    for r in range(NSA_GQA):
        lo, hi = r * Q_BLOCK, (r + 1) * Q_BLOCK
        mix_t = (gate_t[3 * r:3 * r + 1, :] * oc_ref[:, lo:hi]
                 + gate_t[3 * r + 1:3 * r + 2, :] * o_sel[:, lo:hi]
                 + gate_t[3 * r + 2:3 * r + 3, :] * o_win[:, lo:hi])
        o_ref[:, lo:hi] = mix_t.T


def _sel_win_attention(proj3, k_aug, vt_sel, k_win, vt_win, negsel, o_cmp, gates, bias_near, bias_win, n_groups):
    b, s, _ = proj3.shape
    width = NSA_GQA * HEAD_DIM
    rows = NSA_GQA * Q_BLOCK
    sp_sel = k_aug.shape[1]
    sp_win = k_win.shape[1]
    nb_sel = vt_sel.shape[2]
    nb_win = vt_win.shape[2]
    return pl.pallas_call(
        _sel_win_kernel,
        grid=(b, n_groups, s // Q_BLOCK),
        in_specs=[pl.BlockSpec((None, Q_BLOCK, width), lambda bi, g, i: (bi, i, g)),
                  pl.BlockSpec((None, sp_sel, 2 * HEAD_DIM), lambda bi, g, i: (bi, 0, g)),
                  pl.BlockSpec((None, None, nb_sel, HEAD_DIM + ONES_ROWS, Q_BLOCK), lambda bi, g, i: (bi, g, 0, 0, 0)),
                  pl.BlockSpec((None, sp_win, HEAD_DIM), lambda bi, g, i: (bi, 0, g)),
                  pl.BlockSpec((None, None, nb_win, HEAD_DIM + ONES_ROWS, Q_BLOCK), lambda bi, g, i: (bi, g, 0, 0, 0)),
                  pl.BlockSpec((None, None, Q_BLOCK, LANES), lambda bi, g, i: (bi, g, i, 0)),
                  pl.BlockSpec((None, None, None, HEAD_DIM, width), lambda bi, g, i: (bi, g, i, 0, 0)),
                  pl.BlockSpec((None, Q_BLOCK, LANES), lambda bi, g, i: (bi, i, g)),
                  pl.BlockSpec((None, NEAR, rows), lambda bi, g, i: (g, 0, 0)),
                  pl.BlockSpec((None, Q_BLOCK + WINDOW, rows), lambda bi, g, i: (g, 0, 0))],
        out_specs=pl.BlockSpec((None, Q_BLOCK, width), lambda bi, g, i: (bi, i, g)),
        out_shape=jax.ShapeDtypeStruct((b, s, n_groups * width), F32),
        scratch_shapes=[pltpu.VMEM((1, rows), F32), pltpu.VMEM((HEAD_DIM + ONES_ROWS, rows), F32)],
        compiler_params=_cparams(("parallel", "parallel", "arbitrary")),
    )(proj3, k_aug, vt_sel, k_win, vt_win, negsel, o_cmp, gates, bias_near, bias_win)


def _router_kernel(h_ref, w_ref, o_ref, cnt_ref, *, n_experts):
    @pl.when(pl.program_id(0) == 0)
    def _():
        cnt_ref[...] = jnp.zeros_like(cnt_ref)

    logits = jnp.dot(h_ref[...], w_ref[...], preferred_element_type=F32)
    lane = lax.broadcasted_iota(jnp.int32, logits.shape, 1).astype(F32)
    logits = jnp.where(lane < float(n_experts), logits, -jnp.inf)
    m1 = jnp.max(logits, axis=1, keepdims=True)
    i1 = jnp.min(jnp.where(logits == m1, lane, float(LANES)), axis=1, keepdims=True)
    rest = jnp.where(lane == i1, -jnp.inf, logits)
    m2 = jnp.max(rest, axis=1, keepdims=True)
    i2 = jnp.min(jnp.where(rest == m2, lane, float(LANES)), axis=1, keepdims=True)
    e2 = jnp.exp(m2 - m1)
    denom = 1.0 + e2

    pick1 = jnp.where(lane == i1, 1.0, 0.0)
    pick2 = jnp.where(lane == i2, 1.0, 0.0)
    picks = pick1 + pick2
    tm = picks.shape[0]
    row = lax.broadcasted_iota(jnp.int32, (tm, tm), 0)
    col = lax.broadcasted_iota(jnp.int32, (tm, tm), 1)
    earlier = jnp.where(col < row, 1.0, 0.0).astype(BF16)
    before = jnp.dot(earlier, picks.astype(BF16), preferred_element_type=F32) + cnt_ref[0:1, :]
    rank1 = jnp.sum(before * pick1, axis=1, keepdims=True)
    rank2 = jnp.sum(before * pick2, axis=1, keepdims=True)
    cnt_ref[...] = cnt_ref[...] + jnp.sum(picks, axis=0, keepdims=True)

    o_ref[...] = (jnp.where(lane == 0.0, i1, 0.0) + jnp.where(lane == 1.0, i2, 0.0)
                  + jnp.where(lane == 2.0, 1.0 / denom, 0.0) + jnp.where(lane == 3.0, e2 / denom, 0.0)
                  + jnp.where(lane == 4.0, rank1, 0.0) + jnp.where(lane == 5.0, rank2, 0.0))


def _router(h, w_router_padded, n_experts):
    t, d = h.shape
    tm = _pick(t, (512, 256, 128))
    return pl.pallas_call(
        functools.partial(_router_kernel, n_experts=n_experts),
        grid=(t // tm,),
        in_specs=[pl.BlockSpec((tm, d), lambda i: (i, 0)), pl.BlockSpec((d, LANES), lambda i: (0, 0))],
        out_specs=[pl.BlockSpec((tm, LANES), lambda i: (i, 0)), pl.BlockSpec((8, LANES), lambda i: (0, 0))],
        out_shape=[jax.ShapeDtypeStruct((t, LANES), F32), jax.ShapeDtypeStruct((8, LANES), F32)],
        compiler_params=_cparams(("arbitrary",)),
    )(h, w_router_padded)


def _row_copy(src_ref, dst_ref, sem, src_row, dst_row):
    return pltpu.make_async_copy(src_ref.at[pl.ds(src_row, 1)], dst_ref.at[pl.ds(dst_row, 1)], sem)


def _gather_rows_kernel(idx_ref, src_ref, out_ref, sem, *, rows):
    base = pl.program_id(0) * rows

    def start(r2, carry):
        for lane in range(2):
            r = 2 * r2 + lane
            _row_copy(src_ref, out_ref, sem, idx_ref[base + r], r).start(priority=lane)
        return carry

    lax.fori_loop(0, rows // 2, start, 0)

    def wait(r, carry):
        _row_copy(src_ref, out_ref, sem, 0, r).wait()
        return carry

    lax.fori_loop(0, rows, wait, 0)


def _gather_rows(src, idx):
    n_out = idx.shape[0]
    n_src, width = src.shape
    rows = _pick(n_out, (GATHER_ROWS, 256, 128))
    src = src.reshape(n_src, width // LANES, LANES)
    out = pl.pallas_call(
        functools.partial(_gather_rows_kernel, rows=rows),
        grid_spec=pltpu.PrefetchScalarGridSpec(
            num_scalar_prefetch=1,
            grid=(n_out // rows,),
            in_specs=[pl.BlockSpec(memory_space=pl.ANY)],
            out_specs=pl.BlockSpec((rows,) + src.shape[1:], lambda c, idx_ref: (c, 0, 0)),
            scratch_shapes=[pltpu.SemaphoreType.DMA(())]),
        out_shape=jax.ShapeDtypeStruct((n_out,) + src.shape[1:], src.dtype),
        compiler_params=_cparams(("arbitrary",)),
    )(idx, src)
    return out.reshape(n_out, width)


def _gmm_kernel(te_ref, nv_ref, x_ref, w_ref, *rest, swiglu):
    o_ref = rest[-1]
    live = pl.program_id(1) < nv_ref[0]

    @pl.when(live)
    def _():
        x = x_ref[...]
        r = jnp.dot(x, w_ref[...].astype(BF16), preferred_element_type=F32)
        if swiglu:
            r = (r * jax.nn.sigmoid(r)) * jnp.dot(x, rest[0][...].astype(BF16), preferred_element_type=F32)
        o_ref[...] = r.astype(o_ref.dtype)

    @pl.when(jnp.logical_not(live))
    def _():
        o_ref[...] = jnp.zeros_like(o_ref)


def _grouped_matmul(x, w, tile_expert, n_valid, *, w2=None, tn=512):
    p, kdim = x.shape
    n = w.shape[2]
    tn = _pick(n, (tn, 256, 128))
    wspec = pl.BlockSpec((None, kdim, tn), lambda j, i, te, nv: (te[i], 0, j))
    in_specs = [pl.BlockSpec((MOE_TM, kdim), lambda j, i, te, nv: (i, 0)), wspec]
    args = [x, w]
    if w2 is not None:
        in_specs.append(wspec)
        args.append(w2)
    return pl.pallas_call(
        functools.partial(_gmm_kernel, swiglu=w2 is not None),
        grid_spec=pltpu.PrefetchScalarGridSpec(
            num_scalar_prefetch=2,
            grid=(n // tn, p // MOE_TM),
            in_specs=in_specs,
            out_specs=pl.BlockSpec((MOE_TM, tn), lambda j, i, te, nv: (i, j))),
        out_shape=jax.ShapeDtypeStruct((p, n), BF16),
        compiler_params=_cparams(("parallel", "parallel")),
    )(tile_expert, n_valid, *args)


def _combine_kernel(x_ref, ya_ref, yb_ref, r_ref, *rest):
    o_ref = rest[-1]
    route = r_ref[...]
    v = x_ref[...] + route[:, 2:3] * ya_ref[...].astype(F32) + route[:, 3:4] * yb_ref[...].astype(F32)
    if len(rest) == 2:
        v = v * lax.rsqrt(jnp.mean(v * v, axis=-1, keepdims=True) + EPS) * rest[0][...]
    o_ref[...] = v


def _moe_combine(x, y2, route, norm_gain=None):
    t, d = x.shape
    tm = _pick(t, (256, 128))
    nt = t // tm
    in_specs = [pl.BlockSpec((tm, d), lambda i: (i, 0)),
                pl.BlockSpec((tm, d), lambda i: (i, 0)),
                pl.BlockSpec((tm, d), lambda i: (i + nt, 0)),
                pl.BlockSpec((tm, LANES), lambda i: (i, 0))]
    args = [x, y2, y2, route]
    if norm_gain is not None:
        in_specs.append(pl.BlockSpec((1, d), lambda i: (0, 0)))
        args.append(norm_gain.reshape(1, d).astype(F32))
    return pl.pallas_call(
        _combine_kernel,
        grid=(nt,),
        in_specs=in_specs,
        out_specs=pl.BlockSpec((tm, d), lambda i: (i, 0)),
        out_shape=jax.ShapeDtypeStruct((t, d), F32),
        compiler_params=_cparams(("parallel",)),
    )(*args)


def _route_plan(route, counts, n_experts):
    t = route.shape[0]
    experts = route[:, 0:2].astype(jnp.int32).T.reshape(-1)
    rank = route[:, 4:6].astype(jnp.int32).T.reshape(-1)
    counts = counts[0, :n_experts].astype(jnp.int32)
    padded = ((counts + MOE_TM - 1) // MOE_TM) * MOE_TM
    ends = jnp.cumsum(padded)
    pos = (ends - padded)[experts] + rank
    p_rows = 2 * t + n_experts * MOE_TM
    token = jnp.tile(jnp.arange(t, dtype=jnp.int32), 2)
    src = jnp.zeros((p_rows,), jnp.int32).at[pos].set(token)
    tile_start = jnp.arange(p_rows // MOE_TM, dtype=jnp.int32) * MOE_TM
    tile_expert = jnp.minimum(jnp.sum((tile_start[:, None] >= ends[None, :]).astype(jnp.int32), axis=1),
                              n_experts - 1)
    n_valid = (ends[-1] // MOE_TM).reshape(1)
    return pos.astype(jnp.int32), src, tile_expert.astype(jnp.int32), n_valid.astype(jnp.int32)


def _moe_ffn(x, h, router_w, wg, wu, wd, norm_gain=None):
    n_experts = router_w.shape[1]
    route, counts = _router(h, jnp.pad(router_w, ((0, 0), (0, LANES - n_experts))).astype(BF16), n_experts)
    pos, src, tile_expert, n_valid = _route_plan(route, counts, n_experts)
    xs = _gather_rows(h, src)
    act = _grouped_matmul(xs, wg, tile_expert, n_valid, w2=wu)
    y = _grouped_matmul(act, wd, tile_expert, n_valid)
    return _moe_combine(x, _gather_rows(y, pos), route, norm_gain)


def _static_tables(s):
    n_sel = s // SEL_BLOCK
    ncp = max(LANES, -(-(s // CMP_STRIDE) // LANES) * LANES)
    n_cmp = (s - CMP_BLOCK) // CMP_STRIDE + 1
    span = Q_BLOCK + WINDOW
    d_near = (NEAR - 1) - np.arange(NEAR + Q_BLOCK - 1)
    d_win = (span - 1) - np.arange(span + Q_BLOCK - 1)
    r = np.arange(Q_BLOCK)[:, None]
    d_cmp = r - CMP_STRIDE * np.arange(CMP_NEAR[0], CMP_NEAR[1])[None, :] - (CMP_BLOCK - 1)
    c0 = np.arange(ncp)[:, None] * CMP_STRIDE
    j0 = np.arange(LANES)[None, :] * SEL_BLOCK
    overlap = np.clip(np.minimum(c0 + CMP_BLOCK, j0 + SEL_BLOCK) - np.maximum(c0, j0), 0, None).astype(np.float32) / CMP_BLOCK
    overlap[n_cmp:, :] = 0.0
    overlap[:, n_sel:] = 0.0
    onehot = (np.arange(s)[:, None] // SEL_BLOCK == np.arange(LANES)[None, :]).astype(np.float32)
    return dict(
        ncp=ncp,
        b_near=_bucket_np(d_near), ok_near=d_near >= 0,
        b_win=_bucket_np(d_win), ok_win=(d_win >= 0) & (d_win < WINDOW),
        b_cmp=_bucket_np(d_cmp), ok_cmp=(d_cmp >= 0) & (d_cmp < FAR_DIST), future_cmp=d_cmp < 0,
        overlap=overlap, onehot=onehot)


def _bias_tables(rel_bias, tabs, n_groups):
    tbl = rel_bias.T.astype(F32) * LOG2E
    reb = tbl - tbl[:, FAR_BUCKET:FAR_BUCKET + 1]
    heads = reb.shape[0]

    def toeplitz(bucket, ok, width):
        seq = jnp.where(jnp.asarray(ok)[None, :], reb[:, jnp.asarray(bucket)], -MASK_BIG)
        period = width + Q_BLOCK
        seq = jnp.pad(seq, ((0, 0), (0, period - seq.shape[1])))
        rolled = jnp.tile(seq, (1, Q_BLOCK + 1))[:, :Q_BLOCK * (period + 1)].reshape(heads, Q_BLOCK, period + 1)
        tile = rolled[:, ::-1, :width]
        tile = tile.reshape(n_groups, NSA_GQA, Q_BLOCK, width).transpose(0, 3, 1, 2)
        return tile.reshape(n_groups, width, NSA_GQA * Q_BLOCK)

    ncp = tabs["ncp"]
    win = jnp.where(jnp.asarray(tabs["ok_cmp"])[None], reb[:, jnp.asarray(tabs["b_cmp"])], 0.0)
    win = jnp.where(jnp.asarray(tabs["future_cmp"])[None], -MASK_BIG, win).transpose(0, 2, 1)
    lo = ncp + CMP_NEAR[0]
    past = jnp.zeros((heads, lo, Q_BLOCK), F32)
    future = jnp.full((heads, 2 * ncp - lo - win.shape[1], Q_BLOCK), -MASK_BIG, F32)
    cmp_tab = jnp.concatenate([past, win, future], axis=1)
    cmp_tab = cmp_tab.reshape(n_groups, NSA_GQA, 2 * ncp, Q_BLOCK).transpose(0, 2, 1, 3)
    return (toeplitz(tabs["b_near"], tabs["ok_near"], NEAR),
            toeplitz(tabs["b_win"], tabs["ok_win"], Q_BLOCK + WINDOW),
            cmp_tab.reshape(n_groups, 2 * ncp, NSA_GQA * Q_BLOCK))


def _key_blocks_t(a, g, front_blocks):
    b, s, _ = a.shape
    a = _with_ones_rows(a.reshape(b, s // Q_BLOCK, Q_BLOCK, g, HEAD_DIM).transpose(0, 3, 1, 4, 2))
    return jnp.pad(a, ((0, 0), (0, 0), (front_blocks, 0), (0, 0), (0, 0)))


def _mixer(h, w_in, f_bias, cmp_pe, cmp_w1, cmp_w2, biases, tabs, b, s):
    d = h.shape[1]
    n_heads = d // HEAD_DIM
    hn = n_heads // 2
    hf = n_heads - hn
    g = hn // NSA_GQA
    gw = g * HEAD_DIM
    qn_w, fox_w = hn * HEAD_DIM, hf * HEAD_DIM
    splits = (qn_w, gw, gw, gw, gw, gw, gw, hn * 3, fox_w, fox_w, fox_w, hf)
    offs = np.concatenate([[0], np.cumsum(splits)])
    col = lambda k: w_in[:, offs[k]:offs[k + 1]]
    scale = HEAD_DIM ** -0.5 * LOG2E

    w_main = jnp.concatenate([col(k) for k in (0, 1, 2, 3, 4, 5, 6, 8, 9, 10)], axis=1).astype(BF16)
    blocks = np.cumsum([0, hn, g, g, g, g, g, g, hf, hf, hf])
    cscale = np.ones((1, w_main.shape[1]), np.float32)
    cscale[0, :qn_w] = scale
    cscale[0, blocks[7] * HEAD_DIM:blocks[8] * HEAD_DIM] = scale
    proj = _matmul(h, w_main, cscale=jnp.asarray(cscale), out_dtype=BF16)
    proj3 = proj.reshape(b, s, -1)
    grp = lambda k: proj3[:, :, blocks[k] * HEAD_DIM:blocks[k + 1] * HEAD_DIM]

    w_gate = jnp.pad(col(7).reshape(d, g, NSA_GQA * 3), ((0, 0), (0, 0), (0, LANES - NSA_GQA * 3))).reshape(d, g * LANES)
    w_f = jnp.pad(col(11), ((0, 0), (0, LANES - hf)))
    small = _matmul(h, jnp.concatenate([w_gate, w_f], axis=1).astype(BF16), tn=(g + 1) * LANES)
    small3 = small.reshape(b, s, -1)
    gates = small3[:, :, :g * LANES]
    f_raw = small3[:, :, g * LANES:]

    cum = _forget_cumsum(f_raw, jnp.pad(f_bias.astype(F32), (0, LANES - hf)).reshape(1, LANES))
    o_fox = _fox_attention(proj3, int(blocks[7]), int(blocks[8]), grp(9).reshape(b, s, hf, HEAD_DIM),
                           cum[:, :, :hf])

    ncp = tabs["ncp"]

    def to_rows(a):
        a = a.reshape(b, s // CMP_STRIDE, CMP_STRIDE, g, HEAD_DIM).transpose(0, 3, 1, 2, 4)
        a = a.reshape(b, g, s // CMP_STRIDE, CMP_STRIDE * HEAD_DIM)
        return jnp.pad(a, ((0, 0), (0, 0), (0, ncp - s // CMP_STRIDE), (0, 0)))

    rows = jnp.stack([to_rows(grp(1)), to_rows(grp(2))])
    half = CMP_STRIDE * HEAD_DIM
    w1cat = jnp.concatenate([cmp_w1[:, :half], cmp_w1[:, half:]], axis=2).astype(BF16)
    pe_flat = jnp.pad(cmp_pe.reshape(2, 1, CMP_BLOCK * HEAD_DIM), ((0, 0), (0, 7), (0, 0))).astype(BF16)
    kv_cmp = _compress(rows, w1cat, pe_flat, cmp_w2.astype(BF16))

    bias_near, bias_win, bias_cmp = biases
    o_cmp, negsel = _cmp_attention(proj3, kv_cmp, bias_cmp, jnp.asarray(tabs["overlap"], BF16), g)

    onehot = jnp.broadcast_to(jnp.asarray(tabs["onehot"], BF16)[None, :, None, :], (b, s, g, LANES))
    k_aug = jnp.concatenate([grp(3).reshape(b, s, g, HEAD_DIM), onehot], axis=-1).reshape(b, s, g * 2 * HEAD_DIM)
    front = lambda a, n: jnp.pad(a, ((0, 0), (n, 0), (0, 0)))
    o_nsa = _sel_win_attention(proj3, front(k_aug, Q_BLOCK), _key_blocks_t(grp(4), g, 1),
                               front(grp(5), WINDOW), _key_blocks_t(grp(6), g, WINDOW // Q_BLOCK),
                               negsel, o_cmp, gates, bias_near, bias_win, g)
    return o_nsa.reshape(b * s, qn_w), o_fox.reshape(b * s, fox_w)


def _pad_to(a, axis, mult):
    n = a.shape[axis]
    target = -(-n // mult) * mult
    if target == n:
        return a
    pad = [(0, 0)] * a.ndim
    pad[axis] = (0, target - n)
    return jnp.pad(a, pad)


def _dense_ffn(x, h, wg, wu, wd):
    ff_tile = 512
    wg = _pad_to(wg, 1, ff_tile).astype(BF16)
    wu = _pad_to(wu, 1, ff_tile).astype(BF16)
    wd = _pad_to(wd, 0, ff_tile).astype(BF16)
    act = _matmul(h, wg, w2=wu, out_dtype=BF16, tn=ff_tile)
    return _matmul(act, wd, res=x, tm=1024, tn=1024, tk=wd.shape[0] // 4)


def kernel(x, attn_norm, w_in, fgate_bias, cmp_pe, cmp_w1, cmp_w2, rel_bias, out_norm_nsa, out_norm_fox,
           w_out, ffn_norm, dense_w_gate, dense_w_up, dense_w_down, router_w, moe_w_gate, moe_w_up,
           moe_w_down, final_norm):
    b, s, d = x.shape
    depth = attn_norm.shape[0]
    n_groups = (d // HEAD_DIM // 2) // NSA_GQA
    assert s % SEL_TK == 0 and s // SEL_BLOCK <= LANES and d % (2 * NSA_GQA * HEAD_DIM) == 0
    assert (b * s) % MOE_TM == 0
    tabs = _static_tables(s)
    biases = _bias_tables(rel_bias, tabs, n_groups)
    xt = x.reshape(b * s, d)
    for layer in range(depth):
        h = _rmsnorm(xt, attn_norm[layer], BF16)
        o_nsa, o_fox = _mixer(h, w_in[layer], fgate_bias[layer], cmp_pe[layer], cmp_w1[layer], cmp_w2[layer],
                              biases, tabs, b, s)
        mixed = _pair_rmsnorm(o_nsa, o_fox, out_norm_nsa[layer], out_norm_fox[layer])
        xt = _matmul(mixed, w_out[layer].astype(BF16), res=xt)
        h = _rmsnorm(xt, ffn_norm[layer], BF16)
        i = layer // 2
        if layer % 2 == 0:
            xt = _dense_ffn(xt, h, dense_w_gate[i], dense_w_up[i], dense_w_down[i])
        elif layer == depth - 1:
            return _moe_ffn(xt, h, router_w[i], moe_w_gate[i], moe_w_up[i], moe_w_down[i],
                            norm_gain=final_norm).astype(x.dtype).reshape(b, s, d)
        else:
            xt = _moe_ffn(xt, h, router_w[i], moe_w_gate[i], moe_w_up[i], moe_w_down[i])
    return _rmsnorm(xt, final_norm, x.dtype).reshape(b, s, d)
```

```python
import functools
import math

import numpy as np
import jax
import jax.numpy as jnp
from jax import lax
from jax.experimental import pallas as pl
from jax.experimental.pallas import tpu as pltpu

F32 = jnp.float32
BF16 = jnp.bfloat16

HEAD_DIM = 128
NSA_GQA = 4
CMP_BLOCK = 32
CMP_STRIDE = 16
SEL_BLOCK = 64
SEL_TOPK = 16
WINDOW = 512
Q_BLOCK = 128
N_BUCKETS = 32
MAX_DISTANCE = 128
TOP_K = 2
EPS = 1e-6
NEG_INF = -1e30
FORCE_BONUS = 1e4

LANES = 128
VMEM_LIMIT = 56 * 1024 * 1024

LOG2E = 1.4426950408889634
MASK_BIG = 2.0 ** 101
REMOVED = -3.0e38
FAR_BUCKET = N_BUCKETS - 1
FAR_DIST = 113
SEL_TK = 512
NEAR = 2 * Q_BLOCK
CMP_NEAR = (-9, 7)
TILES_PER_TRIP = 8
KEY_SPLIT = 2
ONES_ROWS = 16
MOE_TM = 512
GATHER_ROWS = 512

_NT = (((1,), (1,)), ((), ()))


def _cparams(sem):
    return pltpu.CompilerParams(dimension_semantics=sem, vmem_limit_bytes=VMEM_LIMIT)


def _bucket_np(d):
    max_exact = N_BUCKETS // 2
    dd = np.maximum(d, 0)
    ratio = np.log(np.maximum(dd, max_exact).astype(np.float32) / max_exact) / math.log(MAX_DISTANCE / max_exact)
    large = np.minimum(max_exact + (ratio * (N_BUCKETS - max_exact)).astype(np.int32), N_BUCKETS - 1)
    return np.where(dd < max_exact, dd, large).astype(np.int32)


def _pick(n, prefs):
    for p in prefs:
        if n % p == 0:
            return p
    return n


def _rmsnorm_kernel(x_ref, g_ref, o_ref):
    x = x_ref[...].astype(F32)
    y = x * lax.rsqrt(jnp.mean(x * x, axis=-1, keepdims=True) + EPS)
    o_ref[...] = (y * g_ref[...]).astype(o_ref.dtype)


def _rmsnorm(x, gain, out_dtype):
    t, d = x.shape
    tm = _pick(t, (512, 256, 128))
    return pl.pallas_call(
        _rmsnorm_kernel,
        grid=(t // tm,),
        in_specs=[pl.BlockSpec((tm, d), lambda i: (i, 0)), pl.BlockSpec((1, d), lambda i: (0, 0))],
        out_specs=pl.BlockSpec((tm, d), lambda i: (i, 0)),
        out_shape=jax.ShapeDtypeStruct((t, d), out_dtype),
        compiler_params=_cparams(("parallel",)),
    )(x, gain.reshape(1, d).astype(F32))


def _pair_rmsnorm_kernel(a_ref, b_ref, ga_ref, gb_ref, o_ref):
    wa = a_ref.shape[-1]
    for ref, g, lo in ((a_ref, ga_ref, 0), (b_ref, gb_ref, wa)):
        x = ref[...].astype(F32)
        y = x * lax.rsqrt(jnp.mean(x * x, axis=-1, keepdims=True) + EPS)
        o_ref[:, lo:lo + x.shape[-1]] = (y * g[...]).astype(o_ref.dtype)


def _pair_rmsnorm(a, b, ga, gb):
    t, wa = a.shape
    wb = b.shape[1]
    tm = _pick(t, (512, 256, 128))
    return pl.pallas_call(
        _pair_rmsnorm_kernel,
        grid=(t // tm,),
        in_specs=[pl.BlockSpec((tm, wa), lambda i: (i, 0)), pl.BlockSpec((tm, wb), lambda i: (i, 0)),
                  pl.BlockSpec((1, wa), lambda i: (0, 0)), pl.BlockSpec((1, wb), lambda i: (0, 0))],
        out_specs=pl.BlockSpec((tm, wa + wb), lambda i: (i, 0)),
        out_shape=jax.ShapeDtypeStruct((t, wa + wb), BF16),
        compiler_params=_cparams(("parallel",)),
    )(a, b, ga.reshape(1, wa).astype(F32), gb.reshape(1, wb).astype(F32))


def _mm_kernel(*refs, nk, swiglu, has_cscale, has_res):
    it = iter(refs)
    x_ref = next(it)
    w_ref = next(it)
    w2_ref = next(it) if swiglu else None
    cs_ref = next(it) if has_cscale else None
    res_ref = next(it) if has_res else None
    o_ref = next(it)
    acc_ref = next(it) if nk > 1 else None
    acc2_ref = next(it) if (nk > 1 and swiglu) else None

    x = x_ref[...]

    def epilogue(r, r2):
        if swiglu:
            r = (r * jax.nn.sigmoid(r)) * r2
        if has_cscale:
            r = r * cs_ref[...]
        if has_res:
            r = res_ref[...] + r
        o_ref[...] = r.astype(o_ref.dtype)

    if nk == 1:
        r = jnp.dot(x, w_ref[...], preferred_element_type=F32)
        r2 = jnp.dot(x, w2_ref[...], preferred_element_type=F32) if swiglu else None
        epilogue(r, r2)
        return

    k = pl.program_id(2)

    def accum(acc, w):
        part = jnp.dot(x, w[...], preferred_element_type=F32)

        @pl.when(k == 0)
        def _():
            acc[...] = part

        @pl.when(k > 0)
        def _():
            acc[...] += part

    accum(acc_ref, w_ref)
    if swiglu:
        accum(acc2_ref, w2_ref)

    @pl.when(k == nk - 1)
    def _():
        epilogue(acc_ref[...], acc2_ref[...] if swiglu else None)


def _matmul(x, w, *, w2=None, cscale=None, res=None, out_dtype=F32, tm=1024, tn=512, tk=None):
    m, kdim = x.shape
    n = w.shape[1]
    tm = _pick(m, (tm, 512, 256, 128))
    tn = _pick(n, (tn, 512, 256, 128))
    tk = kdim if tk is None else tk
    assert kdim % tk == 0
    nk = kdim // tk
    swiglu = w2 is not None
    in_specs = [pl.BlockSpec((tm, tk), lambda i, j, k: (i, k)),
                pl.BlockSpec((tk, tn), lambda i, j, k: (k, j))]
    args = [x, w]
    if swiglu:
        in_specs.append(pl.BlockSpec((tk, tn), lambda i, j, k: (k, j)))
        args.append(w2)
    if cscale is not None:
        in_specs.append(pl.BlockSpec((1, tn), lambda i, j, k: (0, j)))
        args.append(cscale)
    if res is not None:
        in_specs.append(pl.BlockSpec((tm, tn), lambda i, j, k: (i, j)))
        args.append(res)
    scratch = []
    if nk > 1:
        scratch = [pltpu.VMEM((tm, tn), F32)] * (2 if swiglu else 1)
    kern = functools.partial(_mm_kernel, nk=nk, swiglu=swiglu, has_cscale=cscale is not None,
                             has_res=res is not None)
    return pl.pallas_call(
        kern,
        grid=(m // tm, n // tn, nk),
        in_specs=in_specs,
        out_specs=pl.BlockSpec((tm, tn), lambda i, j, k: (i, j)),
        out_shape=jax.ShapeDtypeStruct((m, n), out_dtype),
        scratch_shapes=scratch,
        compiler_params=_cparams(("parallel", "parallel", "arbitrary")),
    )(*args)


def _cumsum_kernel(f_ref, b_ref, o_ref, carry_ref):
    j = pl.program_id(1)

    @pl.when(j == 0)
    def _():
        carry_ref[...] = jnp.zeros_like(carry_ref)

    z = f_ref[...] + b_ref[...]
    logf = jnp.minimum(z, 0.0) - jnp.log1p(jnp.exp(-jnp.abs(z)))
    ts = z.shape[0]
    row = lax.broadcasted_iota(jnp.int32, (ts, ts), 0)
    col = lax.broadcasted_iota(jnp.int32, (ts, ts), 1)
    tri = jnp.where(col <= row, 1.0, 0.0).astype(F32)
    cum = jnp.dot(tri, logf, preferred_element_type=F32, precision=lax.Precision.HIGHEST)
    cum = cum + carry_ref[0:1, :]
    carry_ref[...] = jnp.broadcast_to(cum[ts - 1:ts, :], carry_ref.shape)
    o_ref[...] = cum


def _forget_cumsum(f_raw, f_bias):
    b, s, _ = f_raw.shape
    ts = _pick(s, (256, 128))
    return pl.pallas_call(
        _cumsum_kernel,
        grid=(b, s // ts),
        in_specs=[pl.BlockSpec((None, ts, LANES), lambda bi, j: (bi, j, 0)),
                  pl.BlockSpec((1, LANES), lambda bi, j: (0, 0))],
        out_specs=pl.BlockSpec((None, ts, LANES), lambda bi, j: (bi, j, 0)),
        out_shape=jax.ShapeDtypeStruct((b, s, LANES), F32),
        scratch_shapes=[pltpu.VMEM((8, LANES), F32)],
        compiler_params=_cparams(("parallel", "arbitrary")),
    )(f_raw, f_bias)


def _online_steps(tiles, m_ref, acc_ref, query_bias=None):
    m = m_ref[...]
    acc = acc_ref[...]
    for st, vt in tiles:
        peak = jnp.max(st, axis=0, keepdims=True)
        if query_bias is not None:
            peak = peak + query_bias
        m_new = jnp.maximum(m, peak)
        alpha = jnp.exp2(m - m_new)
        shift = m_new if query_bias is None else m_new - query_bias
        p = jnp.exp2(st - shift).astype(BF16)
        acc = alpha * acc + jnp.dot(vt, p, preferred_element_type=F32)
        m = m_new
    m_ref[...] = m
    acc_ref[...] = acc


def _sweep_even(n_tiles, step):
    def trip(j, carry):
        step(TILES_PER_TRIP * j, TILES_PER_TRIP)
        return carry

    n_trips = n_tiles // TILES_PER_TRIP
    lax.fori_loop(0, n_trips, trip, 0)
    done = TILES_PER_TRIP * n_trips
    size = TILES_PER_TRIP // 2
    while size >= 2:
        take = ((n_tiles - done) & size) != 0

        @pl.when(take)
        def _(done=done, size=size):
            step(done, size)

        done = done + jnp.where(take, size, 0)
        size //= 2


def _softmax_init(m_ref, acc_ref):
    m_ref[...] = jnp.full_like(m_ref, NEG_INF)
    acc_ref[...] = jnp.zeros_like(acc_ref)


def _normalized(acc):
    return acc[:HEAD_DIM] / acc[HEAD_DIM:HEAD_DIM + 1]


def _with_ones_rows(vt):
    ones = jnp.ones(vt.shape[:-2] + (ONES_ROWS, vt.shape[-1]), vt.dtype)
    return jnp.concatenate([vt, ones], axis=-2)


def _lane_concat(blocks, n):
    return blocks[0] if n == 1 else jnp.concatenate([blocks[j] for j in range(n)], axis=1)


def _fox_kernel(q_ref, k_ref, vt_ref, c_ref, o_ref, m_ref, acc_ref, ck_ref, *, tq):
    qi = pl.program_id(2)
    nq = c_ref.shape[0]

    @pl.when(qi == 0)
    def _():
        for j in range(nq):
            ck_ref[j * tq:(j + 1) * tq, :] = jnp.broadcast_to(c_ref[j:j + 1, :], (LANES, tq)).T

    q = q_ref[...]
    cq = c_ref[pl.ds(qi, 1), :]
    _softmax_init(m_ref, acc_ref)

    def step(ki, diagonals):
        tiles = []
        part = tq // KEY_SPLIT
        for sub, diagonal in enumerate(diagonals):
            vt = vt_ref[ki + sub]
            for piece in range(KEY_SPLIT):
                start = pl.multiple_of((ki + sub) * tq + piece * part, part)
                st = lax.dot_general(k_ref[pl.ds(start, part), :], q, _NT, preferred_element_type=F32)
                ck = ck_ref[pl.ds(start, part), :]
                st = st - jnp.concatenate([ck] * (tq // LANES), axis=1)
                if diagonal:
                    key = lax.broadcasted_iota(jnp.int32, st.shape, 0) + piece * part
                    qry = lax.broadcasted_iota(jnp.int32, st.shape, 1)
                    st = jnp.where(key <= qry, st, -jnp.inf)
                tiles.append((st, vt[:, piece * part:(piece + 1) * part]))
        _online_steps(tiles, m_ref, acc_ref, query_bias=cq)

    _sweep_even(qi, lambda first, count: step(first, (False,) * count))

    @pl.when(qi % 2 == 1)
    def _():
        step(qi - 1, (False, True))

    @pl.when(qi % 2 == 0)
    def _():
        step(qi, (True,))

    o_ref[...] = _normalized(acc_ref[...]).T


def _fox_attention(proj3, q0, k0, v, cum):
    b, s, h, hd = v.shape
    tq = _pick(s, (512, 256, 128))
    nq = s // tq
    c2 = (cum * LOG2E).transpose(0, 2, 1).reshape(b, h, nq, tq)
    vt = _with_ones_rows(v.reshape(b, nq, tq, h, hd).transpose(0, 3, 1, 4, 2))
    hv = hd + ONES_ROWS
    return pl.pallas_call(
        functools.partial(_fox_kernel, tq=tq),
        grid=(b, h, nq),
        in_specs=[pl.BlockSpec((None, tq, hd), lambda bi, hi, qi: (bi, qi, q0 + hi)),
                  pl.BlockSpec((None, s, hd), lambda bi, hi, qi: (bi, 0, k0 + hi)),
                  pl.BlockSpec((None, None, nq, hv, tq), lambda bi, hi, qi: (bi, hi, 0, 0, 0)),
                  pl.BlockSpec((None, None, nq, tq), lambda bi, hi, qi: (bi, hi, 0, 0))],
        out_specs=pl.BlockSpec((None, tq, hd), lambda bi, hi, qi: (bi, qi, hi)),
        out_shape=jax.ShapeDtypeStruct((b, s, h * hd), F32),
        scratch_shapes=[pltpu.VMEM((1, tq), F32), pltpu.VMEM((hv, tq), F32), pltpu.VMEM((s, LANES), F32)],
        compiler_params=_cparams(("parallel", "parallel", "arbitrary")),
    )(proj3, proj3, vt, c2)


def _compress_kernel(r_ref, w1_ref, pe_ref, w2_ref, o_ref):
    ab = jnp.dot(r_ref[...], w1_ref[...], preferred_element_type=F32)
    nc = ab.shape[0]
    half = pe_ref.shape[1] // 2
    first = ab[:, :HEAD_DIM]
    second = pltpu.roll(ab[:, HEAD_DIM:], nc - 1, axis=0)
    pe_term = (jnp.dot(pe_ref[:, :half], w1_ref[:, :HEAD_DIM], preferred_element_type=F32)
               + jnp.dot(pe_ref[:, half:], w1_ref[:, HEAD_DIM:], preferred_element_type=F32))
    pre = first + second + pe_term[0:1, :]
    hid = pre * jax.nn.sigmoid(pre)
    o_ref[...] = jnp.dot(hid.astype(BF16), w2_ref[...], preferred_element_type=F32).astype(o_ref.dtype)


def _compress(rows, w1cat, pe_flat, w2):
    two, b, g, nc, width = rows.shape
    return pl.pallas_call(
        _compress_kernel,
        grid=(two, b, g),
        in_specs=[pl.BlockSpec((None, None, None, nc, width), lambda a, bi, gi: (a, bi, gi, 0, 0)),
                  pl.BlockSpec((None, width, 2 * HEAD_DIM), lambda a, bi, gi: (a, 0, 0)),
                  pl.BlockSpec((None, 8, 2 * width), lambda a, bi, gi: (a, 0, 0)),
                  pl.BlockSpec((None, HEAD_DIM, HEAD_DIM), lambda a, bi, gi: (a, 0, 0))],
        out_specs=pl.BlockSpec((None, None, None, nc, HEAD_DIM), lambda a, bi, gi: (a, bi, gi, 0, 0)),
        out_shape=jax.ShapeDtypeStruct((two, b, g, nc, HEAD_DIM), BF16),
        compiler_params=_cparams(("parallel", "parallel", "parallel")),
    )(rows, w1cat, pe_flat, w2)


def _stack_heads(q_ref):
    return jnp.concatenate([q_ref[:, r * HEAD_DIM:(r + 1) * HEAD_DIM] for r in range(NSA_GQA)], axis=0)


def _cmp_attn_kernel(q_ref, kc_ref, lhs_ref, gd_ref, oc_ref, ns_ref, *, n_sel, top_n):
    i = pl.program_id(2)
    t0 = i * Q_BLOCK
    q4 = _stack_heads(q_ref)
    st = lax.dot_general(kc_ref[...], q4, _NT, preferred_element_type=F32)
    ncp = st.shape[0]
    first = pl.multiple_of(ncp - i * (Q_BLOCK // CMP_STRIDE), Q_BLOCK // CMP_STRIDE)
    st = st + gd_ref[pl.ds(first, ncp), :]
    m = jnp.maximum(jnp.max(st, axis=0, keepdims=True), NEG_INF)
    p = jnp.exp2(st - m).astype(BF16)
    r = jnp.dot(lhs_ref[...], p, preferred_element_type=F32)
    l = r[HEAD_DIM:HEAD_DIM + 1]
    inv = jnp.where(l > 0.0, 1.0 / l, 0.0)
    oc_ref[...] = r[:HEAD_DIM] * inv
    imp4 = r[HEAD_DIM + ONES_ROWS:] * inv
    imp = imp4[:, 0:Q_BLOCK]
    for h in range(1, NSA_GQA):
        imp = imp + imp4[:, h * Q_BLOCK:(h + 1) * Q_BLOCK]

    shape = (LANES, Q_BLOCK)
    blk = lax.broadcasted_iota(jnp.int32, shape, 0).astype(F32)
    cur = ((t0 + lax.broadcasted_iota(jnp.int32, shape, 1)) // SEL_BLOCK).astype(F32)
    visible = blk <= cur
    forced = (blk == 0.0) | (blk == cur) | (blk == cur - 1.0)
    score = jnp.where(visible, imp + jnp.where(forced, FORCE_BONUS, 0.0), NEG_INF)
    score = jnp.where(blk < float(n_sel), score, REMOVED)

    def pick(_, carry):
        score, sel = carry
        mx = jnp.max(score, axis=0, keepdims=True)
        first = jnp.min(jnp.where(score == mx, blk, float(LANES)), axis=0, keepdims=True)
        hit = blk == first
        return jnp.where(hit, REMOVED, score), jnp.where(hit, 1.0, sel)

    _, sel = lax.fori_loop(0, top_n, pick, (score, jnp.zeros(shape, F32)))
    ns_ref[...] = jnp.where((sel > 0.0) & visible, 0.0, -MASK_BIG).T.astype(ns_ref.dtype)


def _cmp_attention(proj3, kv_cmp, gd, overlap, n_groups):
    b, s, _ = proj3.shape
    ncp = kv_cmp.shape[3]
    n_sel = s // SEL_BLOCK
    nq = s // Q_BLOCK
    width = NSA_GQA * HEAD_DIM
    vct = _with_ones_rows(kv_cmp[1].transpose(0, 1, 3, 2))
    lhs = jnp.concatenate([vct, jnp.broadcast_to(overlap.T[None, None], (b, n_groups, LANES, ncp))], axis=2)
    kern = functools.partial(_cmp_attn_kernel, n_sel=n_sel, top_n=min(SEL_TOPK, n_sel))
    return pl.pallas_call(
        kern,
        grid=(b, n_groups, nq),
        in_specs=[pl.BlockSpec((None, Q_BLOCK, width), lambda bi, g, i: (bi, i, g)),
                  pl.BlockSpec((None, None, None, ncp, HEAD_DIM), lambda bi, g, i: (0, bi, g, 0, 0)),
                  pl.BlockSpec((None, None, lhs.shape[2], ncp), lambda bi, g, i: (bi, g, 0, 0)),
                  pl.BlockSpec((None, 2 * ncp, width), lambda bi, g, i: (g, 0, 0))],
        out_specs=[pl.BlockSpec((None, None, None, HEAD_DIM, width), lambda bi, g, i: (bi, g, i, 0, 0)),
                   pl.BlockSpec((None, None, Q_BLOCK, LANES), lambda bi, g, i: (bi, g, i, 0))],
        out_shape=[jax.ShapeDtypeStruct((b, n_groups, nq, HEAD_DIM, width), F32),
                   jax.ShapeDtypeStruct((b, n_groups, s, LANES), BF16)],
        compiler_params=_cparams(("parallel", "parallel", "parallel")),
    )(proj3, kv_cmp, lhs, gd)


def _sel_win_kernel(q_ref, ka_ref, vst_ref, kw_ref, vwt_ref, ns_ref, oc_ref, g_ref, bn_ref, bw_ref,
                    o_ref, m_ref, acc_ref):
    i = pl.program_id(2)
    t0 = pl.multiple_of(i * Q_BLOCK, Q_BLOCK)
    q4 = _stack_heads(q_ref)

    ns = ns_ref[...]
    lane = lax.broadcasted_iota(jnp.int32, ns.shape, 1)
    near_blk = (lane >= 2 * i - 2) & (lane <= 2 * i + 1)
    ns_far = jnp.where(near_blk, -MASK_BIG, ns.astype(F32)).astype(BF16)
    qa_far = jnp.concatenate([q4, jnp.concatenate([ns_far] * NSA_GQA, axis=0)], axis=1)
    qa_near = jnp.concatenate([q4, jnp.concatenate([ns] * NSA_GQA, axis=0)], axis=1)
    _softmax_init(m_ref, acc_ref)

    blocks_per_tile = SEL_TK // Q_BLOCK

    def far_tile(j):
        part = SEL_TK // KEY_SPLIT
        blocks = blocks_per_tile // KEY_SPLIT
        pieces = []
        for piece in range(KEY_SPLIT):
            start = pl.multiple_of(Q_BLOCK + j * SEL_TK + piece * part, Q_BLOCK)
            st = lax.dot_general(ka_ref[pl.ds(start, part), :], qa_far, _NT, preferred_element_type=F32)
            first_block = 1 + j * blocks_per_tile + piece * blocks
            pieces.append((st, _lane_concat(vst_ref[pl.ds(first_block, blocks)], blocks)))
        return pieces

    def near_tile():
        st = lax.dot_general(ka_ref[pl.ds(t0, NEAR), :], qa_near, _NT, preferred_element_type=F32)
        st = st + bn_ref[...]
        key_pos = t0 - Q_BLOCK + lax.broadcasted_iota(jnp.int32, st.shape, 0)
        return jnp.where(key_pos >= 0, st, -jnp.inf), _lane_concat(vst_ref[pl.ds(i, 2)], 2)

    n_far = jnp.where(i == 0, 0, (i + 2) // 4)

    _sweep_even(n_far, lambda first, count: _online_steps([t for sub in range(count) for t in far_tile(first + sub)],
                                                          m_ref, acc_ref))

    @pl.when(n_far % 2 == 1)
    def _():
        _online_steps(far_tile(n_far - 1) + [near_tile()], m_ref, acc_ref)

    @pl.when(n_far % 2 == 0)
    def _():
        _online_steps([near_tile()], m_ref, acc_ref)

    o_sel = _normalized(acc_ref[...])

    span = Q_BLOCK + WINDOW
    sw = lax.dot_general(kw_ref[pl.ds(t0, span), :], q4, _NT, preferred_element_type=F32)
    sw = sw + bw_ref[...]
    key_pos = t0 - WINDOW + lax.broadcasted_iota(jnp.int32, sw.shape, 0)
    sw = jnp.where(key_pos >= 0, sw, -jnp.inf)
    pw = jnp.exp2(sw - jnp.max(sw, axis=0, keepdims=True)).astype(BF16)
    n_blk = span // Q_BLOCK
    o_win = _normalized(jnp.dot(_lane_concat(vwt_ref[pl.ds(i, n_blk)], n_blk), pw,
                                preferred_element_type=F32))

    gate_t = jax.nn.sigmoid(g_ref[...]).T
    for r in range(NSA_GQA):
        lo, hi = r * Q_BLOCK, (r + 1) * Q_BLOCK
        mix_t = (gate_t[3 * r:3 * r + 1, :] * oc_ref[:, lo:hi]
                 + gate_t[3 * r + 1:3 * r + 2, :] * o_sel[:, lo:hi]
                 + gate_t[3 * r + 2:3 * r + 3, :] * o_win[:, lo:hi])
        o_ref[:, lo:hi] = mix_t.T


def _sel_win_attention(proj3, k_aug, vt_sel, k_win, vt_win, negsel, o_cmp, gates, bias_near, bias_win, n_groups):
    b, s, _ = proj3.shape
    width = NSA_GQA * HEAD_DIM
    rows = NSA_GQA * Q_BLOCK
    sp_sel = k_aug.shape[1]
    sp_win = k_win.shape[1]
    nb_sel = vt_sel.shape[2]
    nb_win = vt_win.shape[2]
    return pl.pallas_call(
        _sel_win_kernel,
        grid=(b, n_groups, s // Q_BLOCK),
        in_specs=[pl.BlockSpec((None, Q_BLOCK, width), lambda bi, g, i: (bi, i, g)),
                  pl.BlockSpec((None, sp_sel, 2 * HEAD_DIM), lambda bi, g, i: (bi, 0, g)),
                  pl.BlockSpec((None, None, nb_sel, HEAD_DIM + ONES_ROWS, Q_BLOCK), lambda bi, g, i: (bi, g, 0, 0, 0)),
                  pl.BlockSpec((None, sp_win, HEAD_DIM), lambda bi, g, i: (bi, 0, g)),
                  pl.BlockSpec((None, None, nb_win, HEAD_DIM + ONES_ROWS, Q_BLOCK), lambda bi, g, i: (bi, g, 0, 0, 0)),
                  pl.BlockSpec((None, None, Q_BLOCK, LANES), lambda bi, g, i: (bi, g, i, 0)),
                  pl.BlockSpec((None, None, None, HEAD_DIM, width), lambda bi, g, i: (bi, g, i, 0, 0)),
                  pl.BlockSpec((None, Q_BLOCK, LANES), lambda bi, g, i: (bi, i, g)),
                  pl.BlockSpec((None, NEAR, rows), lambda bi, g, i: (g, 0, 0)),
                  pl.BlockSpec((None, Q_BLOCK + WINDOW, rows), lambda bi, g, i: (g, 0, 0))],
        out_specs=pl.BlockSpec((None, Q_BLOCK, width), lambda bi, g, i: (bi, i, g)),
        out_shape=jax.ShapeDtypeStruct((b, s, n_groups * width), F32),
        scratch_shapes=[pltpu.VMEM((1, rows), F32), pltpu.VMEM((HEAD_DIM + ONES_ROWS, rows), F32)],
        compiler_params=_cparams(("parallel", "parallel", "arbitrary")),
    )(proj3, k_aug, vt_sel, k_win, vt_win, negsel, o_cmp, gates, bias_near, bias_win)


def _router_kernel(h_ref, w_ref, o_ref, cnt_ref, *, n_experts):
    @pl.when(pl.program_id(0) == 0)
    def _():
        cnt_ref[...] = jnp.zeros_like(cnt_ref)

    logits = jnp.dot(h_ref[...], w_ref[...], preferred_element_type=F32)
    lane = lax.broadcasted_iota(jnp.int32, logits.shape, 1).astype(F32)
    logits = jnp.where(lane < float(n_experts), logits, -jnp.inf)
    m1 = jnp.max(logits, axis=1, keepdims=True)
    i1 = jnp.min(jnp.where(logits == m1, lane, float(LANES)), axis=1, keepdims=True)
    rest = jnp.where(lane == i1, -jnp.inf, logits)
    m2 = jnp.max(rest, axis=1, keepdims=True)
    i2 = jnp.min(jnp.where(rest == m2, lane, float(LANES)), axis=1, keepdims=True)
    e2 = jnp.exp(m2 - m1)
    denom = 1.0 + e2

    pick1 = jnp.where(lane == i1, 1.0, 0.0)
    pick2 = jnp.where(lane == i2, 1.0, 0.0)
    picks = pick1 + pick2
    tm = picks.shape[0]
    row = lax.broadcasted_iota(jnp.int32, (tm, tm), 0)
    col = lax.broadcasted_iota(jnp.int32, (tm, tm), 1)
    earlier = jnp.where(col < row, 1.0, 0.0).astype(BF16)
    before = jnp.dot(earlier, picks.astype(BF16), preferred_element_type=F32) + cnt_ref[0:1, :]
    rank1 = jnp.sum(before * pick1, axis=1, keepdims=True)
    rank2 = jnp.sum(before * pick2, axis=1, keepdims=True)
    cnt_ref[...] = cnt_ref[...] + jnp.sum(picks, axis=0, keepdims=True)

    o_ref[...] = (jnp.where(lane == 0.0, i1, 0.0) + jnp.where(lane == 1.0, i2, 0.0)
                  + jnp.where(lane == 2.0, 1.0 / denom, 0.0) + jnp.where(lane == 3.0, e2 / denom, 0.0)
                  + jnp.where(lane == 4.0, rank1, 0.0) + jnp.where(lane == 5.0, rank2, 0.0))


def _router(h, w_router_padded, n_experts):
    t, d = h.shape
    tm = _pick(t, (512, 256, 128))
    return pl.pallas_call(
        functools.partial(_router_kernel, n_experts=n_experts),
        grid=(t // tm,),
        in_specs=[pl.BlockSpec((tm, d), lambda i: (i, 0)), pl.BlockSpec((d, LANES), lambda i: (0, 0))],
        out_specs=[pl.BlockSpec((tm, LANES), lambda i: (i, 0)), pl.BlockSpec((8, LANES), lambda i: (0, 0))],
        out_shape=[jax.ShapeDtypeStruct((t, LANES), F32), jax.ShapeDtypeStruct((8, LANES), F32)],
        compiler_params=_cparams(("arbitrary",)),
    )(h, w_router_padded)


def _row_copy(src_ref, dst_ref, sem, src_row, dst_row):
    return pltpu.make_async_copy(src_ref.at[pl.ds(src_row, 1)], dst_ref.at[pl.ds(dst_row, 1)], sem)


def _gather_rows_kernel(idx_ref, src_ref, out_ref, sem, *, rows):
    base = pl.program_id(0) * rows

    def start(r2, carry):
        for lane in range(2):
            r = 2 * r2 + lane
            _row_copy(src_ref, out_ref, sem, idx_ref[base + r], r).start(priority=lane)
        return carry

    lax.fori_loop(0, rows // 2, start, 0)

    def wait(r, carry):
        _row_copy(src_ref, out_ref, sem, 0, r).wait()
        return carry

    lax.fori_loop(0, rows, wait, 0)


def _gather_rows(src, idx):
    n_out = idx.shape[0]
    n_src, width = src.shape
    rows = _pick(n_out, (GATHER_ROWS, 256, 128))
    src = src.reshape(n_src, width // LANES, LANES)
    out = pl.pallas_call(
        functools.partial(_gather_rows_kernel, rows=rows),
        grid_spec=pltpu.PrefetchScalarGridSpec(
            num_scalar_prefetch=1,
            grid=(n_out // rows,),
            in_specs=[pl.BlockSpec(memory_space=pl.ANY)],
            out_specs=pl.BlockSpec((rows,) + src.shape[1:], lambda c, idx_ref: (c, 0, 0)),
            scratch_shapes=[pltpu.SemaphoreType.DMA(())]),
        out_shape=jax.ShapeDtypeStruct((n_out,) + src.shape[1:], src.dtype),
        compiler_params=_cparams(("arbitrary",)),
    )(idx, src)
    return out.reshape(n_out, width)


def _gmm_kernel(te_ref, nv_ref, x_ref, w_ref, *rest, swiglu):
    o_ref = rest[-1]
    live = pl.program_id(1) < nv_ref[0]

    @pl.when(live)
    def _():
        x = x_ref[...]
        r = jnp.dot(x, w_ref[...].astype(BF16), preferred_element_type=F32)
        if swiglu:
            r = (r * jax.nn.sigmoid(r)) * jnp.dot(x, rest[0][...].astype(BF16), preferred_element_type=F32)
        o_ref[...] = r.astype(o_ref.dtype)

    @pl.when(jnp.logical_not(live))
    def _():
        o_ref[...] = jnp.zeros_like(o_ref)


def _grouped_matmul(x, w, tile_expert, n_valid, *, w2=None, tn=512):
    p, kdim = x.shape
    n = w.shape[2]
    tn = _pick(n, (tn, 256, 128))
    wspec = pl.BlockSpec((None, kdim, tn), lambda j, i, te, nv: (te[i], 0, j))
    in_specs = [pl.BlockSpec((MOE_TM, kdim), lambda j, i, te, nv: (i, 0)), wspec]
    args = [x, w]
    if w2 is not None:
        in_specs.append(wspec)
        args.append(w2)
    return pl.pallas_call(
        functools.partial(_gmm_kernel, swiglu=w2 is not None),
        grid_spec=pltpu.PrefetchScalarGridSpec(
            num_scalar_prefetch=2,
            grid=(n // tn, p // MOE_TM),
            in_specs=in_specs,
            out_specs=pl.BlockSpec((MOE_TM, tn), lambda j, i, te, nv: (i, j))),
        out_shape=jax.ShapeDtypeStruct((p, n), BF16),
        compiler_params=_cparams(("parallel", "parallel")),
    )(tile_expert, n_valid, *args)


def _combine_kernel(x_ref, ya_ref, yb_ref, r_ref, *rest):
    o_ref = rest[-1]
    route = r_ref[...]
    v = x_ref[...] + route[:, 2:3] * ya_ref[...].astype(F32) + route[:, 3:4] * yb_ref[...].astype(F32)
    if len(rest) == 2:
        v = v * lax.rsqrt(jnp.mean(v * v, axis=-1, keepdims=True) + EPS) * rest[0][...]
    o_ref[...] = v


def _moe_combine(x, y2, route, norm_gain=None):
    t, d = x.shape
    tm = _pick(t, (256, 128))
    nt = t // tm
    in_specs = [pl.BlockSpec((tm, d), lambda i: (i, 0)),
                pl.BlockSpec((tm, d), lambda i: (i, 0)),
                pl.BlockSpec((tm, d), lambda i: (i + nt, 0)),
                pl.BlockSpec((tm, LANES), lambda i: (i, 0))]
    args = [x, y2, y2, route]
    if norm_gain is not None:
        in_specs.append(pl.BlockSpec((1, d), lambda i: (0, 0)))
        args.append(norm_gain.reshape(1, d).astype(F32))
    return pl.pallas_call(
        _combine_kernel,
        grid=(nt,),
        in_specs=in_specs,
        out_specs=pl.BlockSpec((tm, d), lambda i: (i, 0)),
        out_shape=jax.ShapeDtypeStruct((t, d), F32),
        compiler_params=_cparams(("parallel",)),
    )(*args)


def _route_plan(route, counts, n_experts):
    t = route.shape[0]
    experts = route[:, 0:2].astype(jnp.int32).T.reshape(-1)
    rank = route[:, 4:6].astype(jnp.int32).T.reshape(-1)
    counts = counts[0, :n_experts].astype(jnp.int32)
    padded = ((counts + MOE_TM - 1) // MOE_TM) * MOE_TM
    ends = jnp.cumsum(padded)
    pos = (ends - padded)[experts] + rank
    p_rows = 2 * t + n_experts * MOE_TM
    token = jnp.tile(jnp.arange(t, dtype=jnp.int32), 2)
    src = jnp.zeros((p_rows,), jnp.int32).at[pos].set(token)
    tile_start = jnp.arange(p_rows // MOE_TM, dtype=jnp.int32) * MOE_TM
    tile_expert = jnp.minimum(jnp.sum((tile_start[:, None] >= ends[None, :]).astype(jnp.int32), axis=1),
                              n_experts - 1)
    n_valid = (ends[-1] // MOE_TM).reshape(1)
    return pos.astype(jnp.int32), src, tile_expert.astype(jnp.int32), n_valid.astype(jnp.int32)


def _moe_ffn(x, h, router_w, wg, wu, wd, norm_gain=None):
    n_experts = router_w.shape[1]
    route, counts = _router(h, jnp.pad(router_w, ((0, 0), (0, LANES - n_experts))).astype(BF16), n_experts)
    pos, src, tile_expert, n_valid = _route_plan(route, counts, n_experts)
    xs = _gather_rows(h, src)
    act = _grouped_matmul(xs, wg, tile_expert, n_valid, w2=wu)
    y = _grouped_matmul(act, wd, tile_expert, n_valid)
    return _moe_combine(x, _gather_rows(y, pos), route, norm_gain)


def _static_tables(s):
    n_sel = s // SEL_BLOCK
    ncp = max(LANES, -(-(s // CMP_STRIDE) // LANES) * LANES)
    n_cmp = (s - CMP_BLOCK) // CMP_STRIDE + 1
    span = Q_BLOCK + WINDOW
    d_near = (NEAR - 1) - np.arange(NEAR + Q_BLOCK - 1)
    d_win = (span - 1) - np.arange(span + Q_BLOCK - 1)
    r = np.arange(Q_BLOCK)[:, None]
    d_cmp = r - CMP_STRIDE * np.arange(CMP_NEAR[0], CMP_NEAR[1])[None, :] - (CMP_BLOCK - 1)
    c0 = np.arange(ncp)[:, None] * CMP_STRIDE
    j0 = np.arange(LANES)[None, :] * SEL_BLOCK
    overlap = np.clip(np.minimum(c0 + CMP_BLOCK, j0 + SEL_BLOCK) - np.maximum(c0, j0), 0, None).astype(np.float32) / CMP_BLOCK
    overlap[n_cmp:, :] = 0.0
    overlap[:, n_sel:] = 0.0
    onehot = (np.arange(s)[:, None] // SEL_BLOCK == np.arange(LANES)[None, :]).astype(np.float32)
    return dict(
        ncp=ncp,
        b_near=_bucket_np(d_near), ok_near=d_near >= 0,
        b_win=_bucket_np(d_win), ok_win=(d_win >= 0) & (d_win < WINDOW),
        b_cmp=_bucket_np(d_cmp), ok_cmp=(d_cmp >= 0) & (d_cmp < FAR_DIST), future_cmp=d_cmp < 0,
        overlap=overlap, onehot=onehot)


def _bias_tables(rel_bias, tabs, n_groups):
    tbl = rel_bias.T.astype(F32) * LOG2E
    reb = tbl - tbl[:, FAR_BUCKET:FAR_BUCKET + 1]
    heads = reb.shape[0]

    def toeplitz(bucket, ok, width):
        seq = jnp.where(jnp.asarray(ok)[None, :], reb[:, jnp.asarray(bucket)], -MASK_BIG)
        period = width + Q_BLOCK
        seq = jnp.pad(seq, ((0, 0), (0, period - seq.shape[1])))
        rolled = jnp.tile(seq, (1, Q_BLOCK + 1))[:, :Q_BLOCK * (period + 1)].reshape(heads, Q_BLOCK, period + 1)
        tile = rolled[:, ::-1, :width]
        tile = tile.reshape(n_groups, NSA_GQA, Q_BLOCK, width).transpose(0, 3, 1, 2)
        return tile.reshape(n_groups, width, NSA_GQA * Q_BLOCK)

    ncp = tabs["ncp"]
    win = jnp.where(jnp.asarray(tabs["ok_cmp"])[None], reb[:, jnp.asarray(tabs["b_cmp"])], 0.0)
    win = jnp.where(jnp.asarray(tabs["future_cmp"])[None], -MASK_BIG, win).transpose(0, 2, 1)
    lo = ncp + CMP_NEAR[0]
    past = jnp.zeros((heads, lo, Q_BLOCK), F32)
    future = jnp.full((heads, 2 * ncp - lo - win.shape[1], Q_BLOCK), -MASK_BIG, F32)
    cmp_tab = jnp.concatenate([past, win, future], axis=1)
    cmp_tab = cmp_tab.reshape(n_groups, NSA_GQA, 2 * ncp, Q_BLOCK).transpose(0, 2, 1, 3)
    return (toeplitz(tabs["b_near"], tabs["ok_near"], NEAR),
            toeplitz(tabs["b_win"], tabs["ok_win"], Q_BLOCK + WINDOW),
            cmp_tab.reshape(n_groups, 2 * ncp, NSA_GQA * Q_BLOCK))


def _key_blocks_t(a, g, front_blocks):
    b, s, _ = a.shape
    a = _with_ones_rows(a.reshape(b, s // Q_BLOCK, Q_BLOCK, g, HEAD_DIM).transpose(0, 3, 1, 4, 2))
    return jnp.pad(a, ((0, 0), (0, 0), (front_blocks, 0), (0, 0), (0, 0)))


def _mixer(h, w_in, f_bias, cmp_pe, cmp_w1, cmp_w2, biases, tabs, b, s):
    d = h.shape[1]
    n_heads = d // HEAD_DIM
    hn = n_heads // 2
    hf = n_heads - hn
    g = hn // NSA_GQA
    gw = g * HEAD_DIM
    qn_w, fox_w = hn * HEAD_DIM, hf * HEAD_DIM
    splits = (qn_w, gw, gw, gw, gw, gw, gw, hn * 3, fox_w, fox_w, fox_w, hf)
    offs = np.concatenate([[0], np.cumsum(splits)])
    col = lambda k: w_in[:, offs[k]:offs[k + 1]]
    scale = HEAD_DIM ** -0.5 * LOG2E

    w_main = jnp.concatenate([col(k) for k in (0, 1, 2, 3, 4, 5, 6, 8, 9, 10)], axis=1).astype(BF16)
    blocks = np.cumsum([0, hn, g, g, g, g, g, g, hf, hf, hf])
    cscale = np.ones((1, w_main.shape[1]), np.float32)
    cscale[0, :qn_w] = scale
    cscale[0, blocks[7] * HEAD_DIM:blocks[8] * HEAD_DIM] = scale
    proj = _matmul(h, w_main, cscale=jnp.asarray(cscale), out_dtype=BF16)
    proj3 = proj.reshape(b, s, -1)
    grp = lambda k: proj3[:, :, blocks[k] * HEAD_DIM:blocks[k + 1] * HEAD_DIM]

    w_gate = jnp.pad(col(7).reshape(d, g, NSA_GQA * 3), ((0, 0), (0, 0), (0, LANES - NSA_GQA * 3))).reshape(d, g * LANES)
    w_f = jnp.pad(col(11), ((0, 0), (0, LANES - hf)))
    small = _matmul(h, jnp.concatenate([w_gate, w_f], axis=1).astype(BF16), tn=(g + 1) * LANES)
    small3 = small.reshape(b, s, -1)
    gates = small3[:, :, :g * LANES]
    f_raw = small3[:, :, g * LANES:]

    cum = _forget_cumsum(f_raw, jnp.pad(f_bias.astype(F32), (0, LANES - hf)).reshape(1, LANES))
    o_fox = _fox_attention(proj3, int(blocks[7]), int(blocks[8]), grp(9).reshape(b, s, hf, HEAD_DIM),
                           cum[:, :, :hf])

    ncp = tabs["ncp"]

    def to_rows(a):
        a = a.reshape(b, s // CMP_STRIDE, CMP_STRIDE, g, HEAD_DIM).transpose(0, 3, 1, 2, 4)
        a = a.reshape(b, g, s // CMP_STRIDE, CMP_STRIDE * HEAD_DIM)
        return jnp.pad(a, ((0, 0), (0, 0), (0, ncp - s // CMP_STRIDE), (0, 0)))

    rows = jnp.stack([to_rows(grp(1)), to_rows(grp(2))])
    half = CMP_STRIDE * HEAD_DIM
    w1cat = jnp.concatenate([cmp_w1[:, :half], cmp_w1[:, half:]], axis=2).astype(BF16)
    pe_flat = jnp.pad(cmp_pe.reshape(2, 1, CMP_BLOCK * HEAD_DIM), ((0, 0), (0, 7), (0, 0))).astype(BF16)
    kv_cmp = _compress(rows, w1cat, pe_flat, cmp_w2.astype(BF16))

    bias_near, bias_win, bias_cmp = biases
    o_cmp, negsel = _cmp_attention(proj3, kv_cmp, bias_cmp, jnp.asarray(tabs["overlap"], BF16), g)

    onehot = jnp.broadcast_to(jnp.asarray(tabs["onehot"], BF16)[None, :, None, :], (b, s, g, LANES))
    k_aug = jnp.concatenate([grp(3).reshape(b, s, g, HEAD_DIM), onehot], axis=-1).reshape(b, s, g * 2 * HEAD_DIM)
    front = lambda a, n: jnp.pad(a, ((0, 0), (n, 0), (0, 0)))
    o_nsa = _sel_win_attention(proj3, front(k_aug, Q_BLOCK), _key_blocks_t(grp(4), g, 1),
                               front(grp(5), WINDOW), _key_blocks_t(grp(6), g, WINDOW // Q_BLOCK),
                               negsel, o_cmp, gates, bias_near, bias_win, g)
    return o_nsa.reshape(b * s, qn_w), o_fox.reshape(b * s, fox_w)


def _pad_to(a, axis, mult):
    n = a.shape[axis]
    target = -(-n // mult) * mult
    if target == n:
        return a
    pad = [(0, 0)] * a.ndim
    pad[axis] = (0, target - n)
    return jnp.pad(a, pad)


def _dense_ffn(x, h, wg, wu, wd):
    ff_tile = 512
    wg = _pad_to(wg, 1, ff_tile).astype(BF16)
    wu = _pad_to(wu, 1, ff_tile).astype(BF16)
    wd = _pad_to(wd, 0, ff_tile).astype(BF16)
    act = _matmul(h, wg, w2=wu, out_dtype=BF16, tn=ff_tile)
    return _matmul(act, wd, res=x, tm=1024, tn=1024, tk=wd.shape[0] // 4)


def kernel(x, attn_norm, w_in, fgate_bias, cmp_pe, cmp_w1, cmp_w2, rel_bias, out_norm_nsa, out_norm_fox,
           w_out, ffn_norm, dense_w_gate, dense_w_up, dense_w_down, router_w, moe_w_gate, moe_w_up,
           moe_w_down, final_norm):
    b, s, d = x.shape
    depth = attn_norm.shape[0]
    n_groups = (d // HEAD_DIM // 2) // NSA_GQA
    assert s % SEL_TK == 0 and s // SEL_BLOCK <= LANES and d % (2 * NSA_GQA * HEAD_DIM) == 0
    assert (b * s) % MOE_TM == 0
    tabs = _static_tables(s)
    biases = _bias_tables(rel_bias, tabs, n_groups)
    xt = x.reshape(b * s, d)
    for layer in range(depth):
        h = _rmsnorm(xt, attn_norm[layer], BF16)
        o_nsa, o_fox = _mixer(h, w_in[layer], fgate_bias[layer], cmp_pe[layer], cmp_w1[layer], cmp_w2[layer],
                              biases, tabs, b, s)
        mixed = _pair_rmsnorm(o_nsa, o_fox, out_norm_nsa[layer], out_norm_fox[layer])
        xt = _matmul(mixed, w_out[layer].astype(BF16), res=xt)
        h = _rmsnorm(xt, ffn_norm[layer], BF16)
        i = layer // 2
        if layer % 2 == 0:
            xt = _dense_ffn(xt, h, dense_w_gate[i], dense_w_up[i], dense_w_down[i])
        elif layer == depth - 1:
            return _moe_ffn(xt, h, router_w[i], moe_w_gate[i], moe_w_up[i], moe_w_down[i],
                            norm_gain=final_norm).astype(x.dtype).reshape(b, s, d)
        else:
            xt = _moe_ffn(xt, h, router_w[i], moe_w_gate[i], moe_w_up[i], moe_w_down[i])
    return _rmsnorm(xt, final_norm, x.dtype).reshape(b, s, d)
```

```python
import functools
import math

import numpy as np
import jax
import jax.numpy as jnp
from jax import lax
from jax.experimental import pallas as pl
from jax.experimental.pallas import tpu as pltpu

F32 = jnp.float32
BF16 = jnp.bfloat16

HEAD_DIM = 128
NSA_GQA = 4
CMP_BLOCK = 32
CMP_STRIDE = 16
SEL_BLOCK = 64
SEL_TOPK = 16
WINDOW = 512
Q_BLOCK = 128
N_BUCKETS = 32
MAX_DISTANCE = 128
TOP_K = 2
EPS = 1e-6
NEG_INF = -1e30
FORCE_BONUS = 1e4

LANES = 128
VMEM_LIMIT = 56 * 1024 * 1024

LOG2E = 1.4426950408889634
MASK_BIG = 2.0 ** 101
REMOVED = -3.0e38
FAR_BUCKET = N_BUCKETS - 1
FAR_DIST = 113
SEL_TK = 512
NEAR = 2 * Q_BLOCK
CMP_NEAR = (-9, 7)
TILES_PER_TRIP = 8
KEY_SPLIT = 2
ONES_ROWS = 16
MOE_TM = 512
GATHER_ROWS = 512

_NT = (((1,), (1,)), ((), ()))


def _cparams(sem):
    return pltpu.CompilerParams(dimension_semantics=sem, vmem_limit_bytes=VMEM_LIMIT)


def _bucket_np(d):
    max_exact = N_BUCKETS // 2
    dd = np.maximum(d, 0)
    ratio = np.log(np.maximum(dd, max_exact).astype(np.float32) / max_exact) / math.log(MAX_DISTANCE / max_exact)
    large = np.minimum(max_exact + (ratio * (N_BUCKETS - max_exact)).astype(np.int32), N_BUCKETS - 1)
    return np.where(dd < max_exact, dd, large).astype(np.int32)


def _pick(n, prefs):
    for p in prefs:
        if n % p == 0:
            return p
    return n


def _rmsnorm_kernel(x_ref, g_ref, o_ref):
    x = x_ref[...].astype(F32)
    y = x * lax.rsqrt(jnp.mean(x * x, axis=-1, keepdims=True) + EPS)
    o_ref[...] = (y * g_ref[...]).astype(o_ref.dtype)


def _rmsnorm(x, gain, out_dtype):
    t, d = x.shape
    tm = _pick(t, (512, 256, 128))
    return pl.pallas_call(
        _rmsnorm_kernel,
        grid=(t // tm,),
        in_specs=[pl.BlockSpec((tm, d), lambda i: (i, 0)), pl.BlockSpec((1, d), lambda i: (0, 0))],
        out_specs=pl.BlockSpec((tm, d), lambda i: (i, 0)),
        out_shape=jax.ShapeDtypeStruct((t, d), out_dtype),
        compiler_params=_cparams(("parallel",)),
    )(x, gain.reshape(1, d).astype(F32))


def _pair_rmsnorm_kernel(a_ref, b_ref, ga_ref, gb_ref, o_ref):
    wa = a_ref.shape[-1]
    for ref, g, lo in ((a_ref, ga_ref, 0), (b_ref, gb_ref, wa)):
        x = ref[...].astype(F32)
        y = x * lax.rsqrt(jnp.mean(x * x, axis=-1, keepdims=True) + EPS)
        o_ref[:, lo:lo + x.shape[-1]] = (y * g[...]).astype(o_ref.dtype)


def _pair_rmsnorm(a, b, ga, gb):
    t, wa = a.shape
    wb = b.shape[1]
    tm = _pick(t, (512, 256, 128))
    return pl.pallas_call(
        _pair_rmsnorm_kernel,
        grid=(t // tm,),
        in_specs=[pl.BlockSpec((tm, wa), lambda i: (i, 0)), pl.BlockSpec((tm, wb), lambda i: (i, 0)),
                  pl.BlockSpec((1, wa), lambda i: (0, 0)), pl.BlockSpec((1, wb), lambda i: (0, 0))],
        out_specs=pl.BlockSpec((tm, wa + wb), lambda i: (i, 0)),
        out_shape=jax.ShapeDtypeStruct((t, wa + wb), BF16),
        compiler_params=_cparams(("parallel",)),
    )(a, b, ga.reshape(1, wa).astype(F32), gb.reshape(1, wb).astype(F32))


def _mm_kernel(*refs, nk, swiglu, has_cscale, has_res):
    it = iter(refs)
    x_ref = next(it)
    w_ref = next(it)
    w2_ref = next(it) if swiglu else None
    cs_ref = next(it) if has_cscale else None
    res_ref = next(it) if has_res else None
    o_ref = next(it)
    acc_ref = next(it) if nk > 1 else None
    acc2_ref = next(it) if (nk > 1 and swiglu) else None

    x = x_ref[...]

    def epilogue(r, r2):
        if swiglu:
            r = (r * jax.nn.sigmoid(r)) * r2
        if has_cscale:
            r = r * cs_ref[...]
        if has_res:
            r = res_ref[...] + r
        o_ref[...] = r.astype(o_ref.dtype)

    if nk == 1:
        r = jnp.dot(x, w_ref[...], preferred_element_type=F32)
        r2 = jnp.dot(x, w2_ref[...], preferred_element_type=F32) if swiglu else None
        epilogue(r, r2)
        return

    k = pl.program_id(2)

    def accum(acc, w):
        part = jnp.dot(x, w[...], preferred_element_type=F32)

        @pl.when(k == 0)
        def _():
            acc[...] = part

        @pl.when(k > 0)
        def _():
            acc[...] += part

    accum(acc_ref, w_ref)
    if swiglu:
        accum(acc2_ref, w2_ref)

    @pl.when(k == nk - 1)
    def _():
        epilogue(acc_ref[...], acc2_ref[...] if swiglu else None)


def _matmul(x, w, *, w2=None, cscale=None, res=None, out_dtype=F32, tm=1024, tn=512, tk=None):
    m, kdim = x.shape
    n = w.shape[1]
    tm = _pick(m, (tm, 512, 256, 128))
    tn = _pick(n, (tn, 512, 256, 128))
    tk = kdim if tk is None else tk
    assert kdim % tk == 0
    nk = kdim // tk
    swiglu = w2 is not None
    in_specs = [pl.BlockSpec((tm, tk), lambda i, j, k: (i, k)),
                pl.BlockSpec((tk, tn), lambda i, j, k: (k, j))]
    args = [x, w]
    if swiglu:
        in_specs.append(pl.BlockSpec((tk, tn), lambda i, j, k: (k, j)))
        args.append(w2)
    if cscale is not None:
        in_specs.append(pl.BlockSpec((1, tn), lambda i, j, k: (0, j)))
        args.append(cscale)
    if res is not None:
        in_specs.append(pl.BlockSpec((tm, tn), lambda i, j, k: (i, j)))
        args.append(res)
    scratch = []
    if nk > 1:
        scratch = [pltpu.VMEM((tm, tn), F32)] * (2 if swiglu else 1)
    kern = functools.partial(_mm_kernel, nk=nk, swiglu=swiglu, has_cscale=cscale is not None,
                             has_res=res is not None)
    return pl.pallas_call(
        kern,
        grid=(m // tm, n // tn, nk),
        in_specs=in_specs,
        out_specs=pl.BlockSpec((tm, tn), lambda i, j, k: (i, j)),
        out_shape=jax.ShapeDtypeStruct((m, n), out_dtype),
        scratch_shapes=scratch,
        compiler_params=_cparams(("parallel", "parallel", "arbitrary")),
    )(*args)


def _cumsum_kernel(f_ref, b_ref, o_ref, carry_ref):
    j = pl.program_id(1)

    @pl.when(j == 0)
    def _():
        carry_ref[...] = jnp.zeros_like(carry_ref)

    z = f_ref[...] + b_ref[...]
    logf = jnp.minimum(z, 0.0) - jnp.log1p(jnp.exp(-jnp.abs(z)))
    ts = z.shape[0]
    row = lax.broadcasted_iota(jnp.int32, (ts, ts), 0)
    col = lax.broadcasted_iota(jnp.int32, (ts, ts), 1)
    tri = jnp.where(col <= row, 1.0, 0.0).astype(F32)
    cum = jnp.dot(tri, logf, preferred_element_type=F32, precision=lax.Precision.HIGHEST)
    cum = cum + carry_ref[0:1, :]
    carry_ref[...] = jnp.broadcast_to(cum[ts - 1:ts, :], carry_ref.shape)
    o_ref[...] = cum


def _forget_cumsum(f_raw, f_bias):
    b, s, _ = f_raw.shape
    ts = _pick(s, (256, 128))
    return pl.pallas_call(
        _cumsum_kernel,
        grid=(b, s // ts),
        in_specs=[pl.BlockSpec((None, ts, LANES), lambda bi, j: (bi, j, 0)),
                  pl.BlockSpec((1, LANES), lambda bi, j: (0, 0))],
        out_specs=pl.BlockSpec((None, ts, LANES), lambda bi, j: (bi, j, 0)),
        out_shape=jax.ShapeDtypeStruct((b, s, LANES), F32),
        scratch_shapes=[pltpu.VMEM((8, LANES), F32)],
        compiler_params=_cparams(("parallel", "arbitrary")),
    )(f_raw, f_bias)


def _online_steps(tiles, m_ref, acc_ref, query_bias=None):
    m = m_ref[...]
    acc = acc_ref[...]
    for st, vt in tiles:
        peak = jnp.max(st, axis=0, keepdims=True)
        if query_bias is not None:
            peak = peak + query_bias
        m_new = jnp.maximum(m, peak)
        alpha = jnp.exp2(m - m_new)
        shift = m_new if query_bias is None else m_new - query_bias
        p = jnp.exp2(st - shift).astype(BF16)
        acc = alpha * acc + jnp.dot(vt, p, preferred_element_type=F32)
        m = m_new
    m_ref[...] = m
    acc_ref[...] = acc


def _sweep_even(n_tiles, step):
    def trip(j, carry):
        step(TILES_PER_TRIP * j, TILES_PER_TRIP)
        return carry

    n_trips = n_tiles // TILES_PER_TRIP
    lax.fori_loop(0, n_trips, trip, 0)
    done = TILES_PER_TRIP * n_trips
    size = TILES_PER_TRIP // 2
    while size >= 2:
        take = ((n_tiles - done) & size) != 0

        @pl.when(take)
        def _(done=done, size=size):
            step(done, size)

        done = done + jnp.where(take, size, 0)
        size //= 2


def _softmax_init(m_ref, acc_ref):
    m_ref[...] = jnp.full_like(m_ref, NEG_INF)
    acc_ref[...] = jnp.zeros_like(acc_ref)


def _normalized(acc):
    return acc[:HEAD_DIM] / acc[HEAD_DIM:HEAD_DIM + 1]


def _with_ones_rows(vt):
    ones = jnp.ones(vt.shape[:-2] + (ONES_ROWS, vt.shape[-1]), vt.dtype)
    return jnp.concatenate([vt, ones], axis=-2)


def _lane_concat(blocks, n):
    return blocks[0] if n == 1 else jnp.concatenate([blocks[j] for j in range(n)], axis=1)


def _fox_kernel(q_ref, k_ref, vt_ref, c_ref, o_ref, m_ref, acc_ref, ck_ref, *, tq):
    qi = pl.program_id(2)
    nq = c_ref.shape[0]

    @pl.when(qi == 0)
    def _():
        for j in range(nq):
            ck_ref[j * tq:(j + 1) * tq, :] = jnp.broadcast_to(c_ref[j:j + 1, :], (LANES, tq)).T

    q = q_ref[...]
    cq = c_ref[pl.ds(qi, 1), :]
    _softmax_init(m_ref, acc_ref)

    def step(ki, diagonals):
        tiles = []
        part = tq // KEY_SPLIT
        for sub, diagonal in enumerate(diagonals):
            vt = vt_ref[ki + sub]
            for piece in range(KEY_SPLIT):
                start = pl.multiple_of((ki + sub) * tq + piece * part, part)
                st = lax.dot_general(k_ref[pl.ds(start, part), :], q, _NT, preferred_element_type=F32)
                ck = ck_ref[pl.ds(start, part), :]
                st = st - jnp.concatenate([ck] * (tq // LANES), axis=1)
                if diagonal:
                    key = lax.broadcasted_iota(jnp.int32, st.shape, 0) + piece * part
                    qry = lax.broadcasted_iota(jnp.int32, st.shape, 1)
                    st = jnp.where(key <= qry, st, -jnp.inf)
                tiles.append((st, vt[:, piece * part:(piece + 1) * part]))
        _online_steps(tiles, m_ref, acc_ref, query_bias=cq)

    _sweep_even(qi, lambda first, count: step(first, (False,) * count))

    @pl.when(qi % 2 == 1)
    def _():
        step(qi - 1, (False, True))

    @pl.when(qi % 2 == 0)
    def _():
        step(qi, (True,))

    o_ref[...] = _normalized(acc_ref[...]).T


def _fox_attention(proj3, q0, k0, v, cum):
    b, s, h, hd = v.shape
    tq = _pick(s, (512, 256, 128))
    nq = s // tq
    c2 = (cum * LOG2E).transpose(0, 2, 1).reshape(b, h, nq, tq)
    vt = _with_ones_rows(v.reshape(b, nq, tq, h, hd).transpose(0, 3, 1, 4, 2))
    hv = hd + ONES_ROWS
    return pl.pallas_call(
        functools.partial(_fox_kernel, tq=tq),
        grid=(b, h, nq),
        in_specs=[pl.BlockSpec((None, tq, hd), lambda bi, hi, qi: (bi, qi, q0 + hi)),
                  pl.BlockSpec((None, s, hd), lambda bi, hi, qi: (bi, 0, k0 + hi)),
                  pl.BlockSpec((None, None, nq, hv, tq), lambda bi, hi, qi: (bi, hi, 0, 0, 0)),
                  pl.BlockSpec((None, None, nq, tq), lambda bi, hi, qi: (bi, hi, 0, 0))],
        out_specs=pl.BlockSpec((None, tq, hd), lambda bi, hi, qi: (bi, qi, hi)),
        out_shape=jax.ShapeDtypeStruct((b, s, h * hd), F32),
        scratch_shapes=[pltpu.VMEM((1, tq), F32), pltpu.VMEM((hv, tq), F32), pltpu.VMEM((s, LANES), F32)],
        compiler_params=_cparams(("parallel", "parallel", "arbitrary")),
    )(proj3, proj3, vt, c2)


def _compress_kernel(r_ref, w1_ref, pe_ref, w2_ref, o_ref):
    ab = jnp.dot(r_ref[...], w1_ref[...], preferred_element_type=F32)
    nc = ab.shape[0]
    half = pe_ref.shape[1] // 2
    first = ab[:, :HEAD_DIM]
    second = pltpu.roll(ab[:, HEAD_DIM:], nc - 1, axis=0)
    pe_term = (jnp.dot(pe_ref[:, :half], w1_ref[:, :HEAD_DIM], preferred_element_type=F32)
               + jnp.dot(pe_ref[:, half:], w1_ref[:, HEAD_DIM:], preferred_element_type=F32))
    pre = first + second + pe_term[0:1, :]
    hid = pre * jax.nn.sigmoid(pre)
    o_ref[...] = jnp.dot(hid.astype(BF16), w2_ref[...], preferred_element_type=F32).astype(o_ref.dtype)


def _compress(rows, w1cat, pe_flat, w2):
    two, b, g, nc, width = rows.shape
    return pl.pallas_call(
        _compress_kernel,
        grid=(two, b, g),
        in_specs=[pl.BlockSpec((None, None, None, nc, width), lambda a, bi, gi: (a, bi, gi, 0, 0)),
                  pl.BlockSpec((None, width, 2 * HEAD_DIM), lambda a, bi, gi: (a, 0, 0)),
                  pl.BlockSpec((None, 8, 2 * width), lambda a, bi, gi: (a, 0, 0)),
                  pl.BlockSpec((None, HEAD_DIM, HEAD_DIM), lambda a, bi, gi: (a, 0, 0))],
        out_specs=pl.BlockSpec((None, None, None, nc, HEAD_DIM), lambda a, bi, gi: (a, bi, gi, 0, 0)),
        out_shape=jax.ShapeDtypeStruct((two, b, g, nc, HEAD_DIM), BF16),
        compiler_params=_cparams(("parallel", "parallel", "parallel")),
    )(rows, w1cat, pe_flat, w2)


def _stack_heads(q_ref):
    return jnp.concatenate([q_ref[:, r * HEAD_DIM:(r + 1) * HEAD_DIM] for r in range(NSA_GQA)], axis=0)


def _cmp_attn_kernel(q_ref, kc_ref, lhs_ref, gd_ref, oc_ref, ns_ref, *, n_sel, top_n):
    i = pl.program_id(2)
    t0 = i * Q_BLOCK
    q4 = _stack_heads(q_ref)
    st = lax.dot_general(kc_ref[...], q4, _NT, preferred_element_type=F32)
    ncp = st.shape[0]
    first = pl.multiple_of(ncp - i * (Q_BLOCK // CMP_STRIDE), Q_BLOCK // CMP_STRIDE)
    st = st + gd_ref[pl.ds(first, ncp), :]
    m = jnp.maximum(jnp.max(st, axis=0, keepdims=True), NEG_INF)
    p = jnp.exp2(st - m).astype(BF16)
    r = jnp.dot(lhs_ref[...], p, preferred_element_type=F32)
    l = r[HEAD_DIM:HEAD_DIM + 1]
    inv = jnp.where(l > 0.0, 1.0 / l, 0.0)
    oc_ref[...] = r[:HEAD_DIM] * inv
    imp4 = r[HEAD_DIM + ONES_ROWS:] * inv
    imp = imp4[:, 0:Q_BLOCK]
    for h in range(1, NSA_GQA):
        imp = imp + imp4[:, h * Q_BLOCK:(h + 1) * Q_BLOCK]

    shape = (LANES, Q_BLOCK)
    blk = lax.broadcasted_iota(jnp.int32, shape, 0).astype(F32)
    cur = ((t0 + lax.broadcasted_iota(jnp.int32, shape, 1)) // SEL_BLOCK).astype(F32)
    visible = blk <= cur
    forced = (blk == 0.0) | (blk == cur) | (blk == cur - 1.0)
    score = jnp.where(visible, imp + jnp.where(forced, FORCE_BONUS, 0.0), NEG_INF)
    score = jnp.where(blk < float(n_sel), score, REMOVED)

    def pick(_, carry):
        score, sel = carry
        mx = jnp.max(score, axis=0, keepdims=True)
        first = jnp.min(jnp.where(score == mx, blk, float(LANES)), axis=0, keepdims=True)
        hit = blk == first
        return jnp.where(hit, REMOVED, score), jnp.where(hit, 1.0, sel)

    _, sel = lax.fori_loop(0, top_n, pick, (score, jnp.zeros(shape, F32)))
    ns_ref[...] = jnp.where((sel > 0.0) & visible, 0.0, -MASK_BIG).T.astype(ns_ref.dtype)


def _cmp_attention(proj3, kv_cmp, gd, overlap, n_groups):
    b, s, _ = proj3.shape
    ncp = kv_cmp.shape[3]
    n_sel = s // SEL_BLOCK
    nq = s // Q_BLOCK
    width = NSA_GQA * HEAD_DIM
    vct = _with_ones_rows(kv_cmp[1].transpose(0, 1, 3, 2))
    lhs = jnp.concatenate([vct, jnp.broadcast_to(overlap.T[None, None], (b, n_groups, LANES, ncp))], axis=2)
    kern = functools.partial(_cmp_attn_kernel, n_sel=n_sel, top_n=min(SEL_TOPK, n_sel))
    return pl.pallas_call(
        kern,
        grid=(b, n_groups, nq),
        in_specs=[pl.BlockSpec((None, Q_BLOCK, width), lambda bi, g, i: (bi, i, g)),
                  pl.BlockSpec((None, None, None, ncp, HEAD_DIM), lambda bi, g, i: (0, bi, g, 0, 0)),
                  pl.BlockSpec((None, None, lhs.shape[2], ncp), lambda bi, g, i: (bi, g, 0, 0)),
                  pl.BlockSpec((None, 2 * ncp, width), lambda bi, g, i: (g, 0, 0))],
        out_specs=[pl.BlockSpec((None, None, None, HEAD_DIM, width), lambda bi, g, i: (bi, g, i, 0, 0)),
                   pl.BlockSpec((None, None, Q_BLOCK, LANES), lambda bi, g, i: (bi, g, i, 0))],
        out_shape=[jax.ShapeDtypeStruct((b, n_groups, nq, HEAD_DIM, width), F32),
                   jax.ShapeDtypeStruct((b, n_groups, s, LANES), BF16)],
        compiler_params=_cparams(("parallel", "parallel", "parallel")),
    )(proj3, kv_cmp, lhs, gd)


def _sel_win_kernel(q_ref, ka_ref, vst_ref, kw_ref, vwt_ref, ns_ref, oc_ref, g_ref, bn_ref, bw_ref,
                    o_ref, m_ref, acc_ref):
    i = pl.program_id(2)
    t0 = pl.multiple_of(i * Q_BLOCK, Q_BLOCK)
    q4 = _stack_heads(q_ref)

    ns = ns_ref[...]
    lane = lax.broadcasted_iota(jnp.int32, ns.shape, 1)
    near_blk = (lane >= 2 * i - 2) & (lane <= 2 * i + 1)
    ns_far = jnp.where(near_blk, -MASK_BIG, ns.astype(F32)).astype(BF16)
    qa_far = jnp.concatenate([q4, jnp.concatenate([ns_far] * NSA_GQA, axis=0)], axis=1)
    qa_near = jnp.concatenate([q4, jnp.concatenate([ns] * NSA_GQA, axis=0)], axis=1)
    _softmax_init(m_ref, acc_ref)

    blocks_per_tile = SEL_TK // Q_BLOCK

    def far_tile(j):
        part = SEL_TK // KEY_SPLIT
        blocks = blocks_per_tile // KEY_SPLIT
        pieces = []
        for piece in range(KEY_SPLIT):
            start = pl.multiple_of(Q_BLOCK + j * SEL_TK + piece * part, Q_BLOCK)
            st = lax.dot_general(ka_ref[pl.ds(start, part), :], qa_far, _NT, preferred_element_type=F32)
            first_block = 1 + j * blocks_per_tile + piece * blocks
            pieces.append((st, _lane_concat(vst_ref[pl.ds(first_block, blocks)], blocks)))
        return pieces

    def near_tile():
        st = lax.dot_general(ka_ref[pl.ds(t0, NEAR), :], qa_near, _NT, preferred_element_type=F32)
        st = st + bn_ref[...]
        key_pos = t0 - Q_BLOCK + lax.broadcasted_iota(jnp.int32, st.shape, 0)
        return jnp.where(key_pos >= 0, st, -jnp.inf), _lane_concat(vst_ref[pl.ds(i, 2)], 2)

    n_far = jnp.where(i == 0, 0, (i + 2) // 4)

    _sweep_even(n_far, lambda first, count: _online_steps([t for sub in range(count) for t in far_tile(first + sub)],
                                                          m_ref, acc_ref))

    @pl.when(n_far % 2 == 1)
    def _():
        _online_steps(far_tile(n_far - 1) + [near_tile()], m_ref, acc_ref)

    @pl.when(n_far % 2 == 0)
    def _():
        _online_steps([near_tile()], m_ref, acc_ref)

    o_sel = _normalized(acc_ref[...])

    span = Q_BLOCK + WINDOW
    sw = lax.dot_general(kw_ref[pl.ds(t0, span), :], q4, _NT, preferred_element_type=F32)
    sw = sw + bw_ref[...]
    key_pos = t0 - WINDOW + lax.broadcasted_iota(jnp.int32, sw.shape, 0)
    sw = jnp.where(key_pos >= 0, sw, -jnp.inf)
    pw = jnp.exp2(sw - jnp.max(sw, axis=0, keepdims=True)).astype(BF16)
    n_blk = span // Q_BLOCK
    o_win = _normalized(jnp.dot(_lane_concat(vwt_ref[pl.ds(i, n_blk)], n_blk), pw,
                                preferred_element_type=F32))

    gate_t = jax.nn.sigmoid(g_ref[...]).T
    for r in range(NSA_GQA):
        lo, hi = r * Q_BLOCK, (r + 1) * Q_BLOCK
        mix_t = (gate_t[3 * r:3 * r + 1, :] * oc_ref[:, lo:hi]
                 + gate_t[3 * r + 1:3 * r + 2, :] * o_sel[:, lo:hi]
                 + gate_t[3 * r + 2:3 * r + 3, :] * o_win[:, lo:hi])
        o_ref[:, lo:hi] = mix_t.T


def _sel_win_attention(proj3, k_aug, vt_sel, k_win, vt_win, negsel, o_cmp, gates, bias_near, bias_win, n_groups):
    b, s, _ = proj3.shape
    width = NSA_GQA * HEAD_DIM
    rows = NSA_GQA * Q_BLOCK
    sp_sel = k_aug.shape[1]
    sp_win = k_win.shape[1]
    nb_sel = vt_sel.shape[2]
    nb_win = vt_win.shape[2]
    return pl.pallas_call(
        _sel_win_kernel,
        grid=(b, n_groups, s // Q_BLOCK),
        in_specs=[pl.BlockSpec((None, Q_BLOCK, width), lambda bi, g, i: (bi, i, g)),
                  pl.BlockSpec((None, sp_sel, 2 * HEAD_DIM), lambda bi, g, i: (bi, 0, g)),
                  pl.BlockSpec((None, None, nb_sel, HEAD_DIM + ONES_ROWS, Q_BLOCK), lambda bi, g, i: (bi, g, 0, 0, 0)),
                  pl.BlockSpec((None, sp_win, HEAD_DIM), lambda bi, g, i: (bi, 0, g)),
                  pl.BlockSpec((None, None, nb_win, HEAD_DIM + ONES_ROWS, Q_BLOCK), lambda bi, g, i: (bi, g, 0, 0, 0)),
                  pl.BlockSpec((None, None, Q_BLOCK, LANES), lambda bi, g, i: (bi, g, i, 0)),
                  pl.BlockSpec((None, None, None, HEAD_DIM, width), lambda bi, g, i: (bi, g, i, 0, 0)),
                  pl.BlockSpec((None, Q_BLOCK, LANES), lambda bi, g, i: (bi, i, g)),
                  pl.BlockSpec((None, NEAR, rows), lambda bi, g, i: (g, 0, 0)),
                  pl.BlockSpec((None, Q_BLOCK + WINDOW, rows), lambda bi, g, i: (g, 0, 0))],
        out_specs=pl.BlockSpec((None, Q_BLOCK, width), lambda bi, g, i: (bi, i, g)),
        out_shape=jax.ShapeDtypeStruct((b, s, n_groups * width), F32),
        scratch_shapes=[pltpu.VMEM((1, rows), F32), pltpu.VMEM((HEAD_DIM + ONES_ROWS, rows), F32)],
        compiler_params=_cparams(("parallel", "parallel", "arbitrary")),
    )(proj3, k_aug, vt_sel, k_win, vt_win, negsel, o_cmp, gates, bias_near, bias_win)


def _router_kernel(h_ref, w_ref, o_ref, cnt_ref, *, n_experts):
    @pl.when(pl.program_id(0) == 0)
    def _():
        cnt_ref[...] = jnp.zeros_like(cnt_ref)

    logits = jnp.dot(h_ref[...], w_ref[...], preferred_element_type=F32)
    lane = lax.broadcasted_iota(jnp.int32, logits.shape, 1).astype(F32)
    logits = jnp.where(lane < float(n_experts), logits, -jnp.inf)
    m1 = jnp.max(logits, axis=1, keepdims=True)
    i1 = jnp.min(jnp.where(logits == m1, lane, float(LANES)), axis=1, keepdims=True)
    rest = jnp.where(lane == i1, -jnp.inf, logits)
    m2 = jnp.max(rest, axis=1, keepdims=True)
    i2 = jnp.min(jnp.where(rest == m2, lane, float(LANES)), axis=1, keepdims=True)
    e2 = jnp.exp(m2 - m1)
    denom = 1.0 + e2

    pick1 = jnp.where(lane == i1, 1.0, 0.0)
    pick2 = jnp.where(lane == i2, 1.0, 0.0)
    picks = pick1 + pick2
    tm = picks.shape[0]
    row = lax.broadcasted_iota(jnp.int32, (tm, tm), 0)
    col = lax.broadcasted_iota(jnp.int32, (tm, tm), 1)
    earlier = jnp.where(col < row, 1.0, 0.0).astype(BF16)
    before = jnp.dot(earlier, picks.astype(BF16), preferred_element_type=F32) + cnt_ref[0:1, :]
    rank1 = jnp.sum(before * pick1, axis=1, keepdims=True)
    rank2 = jnp.sum(before * pick2, axis=1, keepdims=True)
    cnt_ref[...] = cnt_ref[...] + jnp.sum(picks, axis=0, keepdims=True)

    o_ref[...] = (jnp.where(lane == 0.0, i1, 0.0) + jnp.where(lane == 1.0, i2, 0.0)
                  + jnp.where(lane == 2.0, 1.0 / denom, 0.0) + jnp.where(lane == 3.0, e2 / denom, 0.0)
                  + jnp.where(lane == 4.0, rank1, 0.0) + jnp.where(lane == 5.0, rank2, 0.0))


def _router(h, w_router_padded, n_experts):
    t, d = h.shape
    tm = _pick(t, (512, 256, 128))
    return pl.pallas_call(
        functools.partial(_router_kernel, n_experts=n_experts),
        grid=(t // tm,),
        in_specs=[pl.BlockSpec((tm, d), lambda i: (i, 0)), pl.BlockSpec((d, LANES), lambda i: (0, 0))],
        out_specs=[pl.BlockSpec((tm, LANES), lambda i: (i, 0)), pl.BlockSpec((8, LANES), lambda i: (0, 0))],
        out_shape=[jax.ShapeDtypeStruct((t, LANES), F32), jax.ShapeDtypeStruct((8, LANES), F32)],
        compiler_params=_cparams(("arbitrary",)),
    )(h, w_router_padded)


def _row_copy(src_ref, dst_ref, sem, src_row, dst_row):
    return pltpu.make_async_copy(src_ref.at[pl.ds(src_row, 1)], dst_ref.at[pl.ds(dst_row, 1)], sem)


def _gather_rows_kernel(idx_ref, src_ref, out_ref, sem, *, rows):
    base = pl.program_id(0) * rows

    def start(r2, carry):
        for lane in range(2):
            r = 2 * r2 + lane
            _row_copy(src_ref, out_ref, sem, idx_ref[base + r], r).start(priority=lane)
        return carry

    lax.fori_loop(0, rows // 2, start, 0)

    pltpu.make_async_copy(src_ref.at[pl.ds(0, rows)], out_ref, sem).wait()


def _gather_rows(src, idx):
    n_out = idx.shape[0]
    n_src, width = src.shape
    rows = _pick(n_out, (GATHER_ROWS, 256, 128))
    src = src.reshape(n_src, width // LANES, LANES)
    out = pl.pallas_call(
        functools.partial(_gather_rows_kernel, rows=rows),
        grid_spec=pltpu.PrefetchScalarGridSpec(
            num_scalar_prefetch=1,
            grid=(n_out // rows,),
            in_specs=[pl.BlockSpec(memory_space=pl.ANY)],
            out_specs=pl.BlockSpec((rows,) + src.shape[1:], lambda c, idx_ref: (c, 0, 0)),
            scratch_shapes=[pltpu.SemaphoreType.DMA(())]),
        out_shape=jax.ShapeDtypeStruct((n_out,) + src.shape[1:], src.dtype),
        compiler_params=_cparams(("arbitrary",)),
    )(idx, src)
    return out.reshape(n_out, width)


def _gmm_kernel(te_ref, nv_ref, x_ref, w_ref, *rest, swiglu):
    o_ref = rest[-1]
    live = pl.program_id(1) < nv_ref[0]

    @pl.when(live)
    def _():
        x = x_ref[...]
        r = jnp.dot(x, w_ref[...].astype(BF16), preferred_element_type=F32)
        if swiglu:
            r = (r * jax.nn.sigmoid(r)) * jnp.dot(x, rest[0][...].astype(BF16), preferred_element_type=F32)
        o_ref[...] = r.astype(o_ref.dtype)

    @pl.when(jnp.logical_not(live))
    def _():
        o_ref[...] = jnp.zeros_like(o_ref)


def _grouped_matmul(x, w, tile_expert, n_valid, *, w2=None, tn=512):
    p, kdim = x.shape
    n = w.shape[2]
    tn = _pick(n, (tn, 256, 128))
    wspec = pl.BlockSpec((None, kdim, tn), lambda j, i, te, nv: (te[i], 0, j))
    in_specs = [pl.BlockSpec((MOE_TM, kdim), lambda j, i, te, nv: (i, 0)), wspec]
    args = [x, w]
    if w2 is not None:
        in_specs.append(wspec)
        args.append(w2)
    return pl.pallas_call(
        functools.partial(_gmm_kernel, swiglu=w2 is not None),
        grid_spec=pltpu.PrefetchScalarGridSpec(
            num_scalar_prefetch=2,
            grid=(n // tn, p // MOE_TM),
            in_specs=in_specs,
            out_specs=pl.BlockSpec((MOE_TM, tn), lambda j, i, te, nv: (i, j))),
        out_shape=jax.ShapeDtypeStruct((p, n), BF16),
        compiler_params=_cparams(("parallel", "parallel")),
    )(tile_expert, n_valid, *args)


def _combine_kernel(x_ref, ya_ref, yb_ref, r_ref, *rest):
    o_ref = rest[-1]
    route = r_ref[...]
    v = x_ref[...] + route[:, 2:3] * ya_ref[...].astype(F32) + route[:, 3:4] * yb_ref[...].astype(F32)
    if len(rest) == 2:
        v = v * lax.rsqrt(jnp.mean(v * v, axis=-1, keepdims=True) + EPS) * rest[0][...]
    o_ref[...] = v


def _moe_combine(x, y2, route, norm_gain=None):
    t, d = x.shape
    tm = _pick(t, (256, 128))
    nt = t // tm
    in_specs = [pl.BlockSpec((tm, d), lambda i: (i, 0)),
                pl.BlockSpec((tm, d), lambda i: (i, 0)),
                pl.BlockSpec((tm, d), lambda i: (i + nt, 0)),
                pl.BlockSpec((tm, LANES), lambda i: (i, 0))]
    args = [x, y2, y2, route]
    if norm_gain is not None:
        in_specs.append(pl.BlockSpec((1, d), lambda i: (0, 0)))
        args.append(norm_gain.reshape(1, d).astype(F32))
    return pl.pallas_call(
        _combine_kernel,
        grid=(nt,),
        in_specs=in_specs,
        out_specs=pl.BlockSpec((tm, d), lambda i: (i, 0)),
        out_shape=jax.ShapeDtypeStruct((t, d), F32),
        compiler_params=_cparams(("parallel",)),
    )(*args)


def _route_plan(route, counts, n_experts):
    t = route.shape[0]
    experts = route[:, 0:2].astype(jnp.int32).T.reshape(-1)
    rank = route[:, 4:6].astype(jnp.int32).T.reshape(-1)
    counts = counts[0, :n_experts].astype(jnp.int32)
    padded = ((counts + MOE_TM - 1) // MOE_TM) * MOE_TM
    ends = jnp.cumsum(padded)
    pos = (ends - padded)[experts] + rank
    p_rows = 2 * t + n_experts * MOE_TM
    token = jnp.tile(jnp.arange(t, dtype=jnp.int32), 2)
    src = jnp.zeros((p_rows,), jnp.int32).at[pos].set(token)
    tile_start = jnp.arange(p_rows // MOE_TM, dtype=jnp.int32) * MOE_TM
    tile_expert = jnp.minimum(jnp.sum((tile_start[:, None] >= ends[None, :]).astype(jnp.int32), axis=1),
                              n_experts - 1)
    n_valid = (ends[-1] // MOE_TM).reshape(1)
    return pos.astype(jnp.int32), src, tile_expert.astype(jnp.int32), n_valid.astype(jnp.int32)


def _moe_ffn(x, h, router_w, wg, wu, wd, norm_gain=None):
    n_experts = router_w.shape[1]
    route, counts = _router(h, jnp.pad(router_w, ((0, 0), (0, LANES - n_experts))).astype(BF16), n_experts)
    pos, src, tile_expert, n_valid = _route_plan(route, counts, n_experts)
    xs = _gather_rows(h, src)
    act = _grouped_matmul(xs, wg, tile_expert, n_valid, w2=wu)
    y = _grouped_matmul(act, wd, tile_expert, n_valid)
    return _moe_combine(x, _gather_rows(y, pos), route, norm_gain)


def _static_tables(s):
    n_sel = s // SEL_BLOCK
    ncp = max(LANES, -(-(s // CMP_STRIDE) // LANES) * LANES)
    n_cmp = (s - CMP_BLOCK) // CMP_STRIDE + 1
    span = Q_BLOCK + WINDOW
    d_near = (NEAR - 1) - np.arange(NEAR + Q_BLOCK - 1)
    d_win = (span - 1) - np.arange(span + Q_BLOCK - 1)
    r = np.arange(Q_BLOCK)[:, None]
    d_cmp = r - CMP_STRIDE * np.arange(CMP_NEAR[0], CMP_NEAR[1])[None, :] - (CMP_BLOCK - 1)
    c0 = np.arange(ncp)[:, None] * CMP_STRIDE
    j0 = np.arange(LANES)[None, :] * SEL_BLOCK
    overlap = np.clip(np.minimum(c0 + CMP_BLOCK, j0 + SEL_BLOCK) - np.maximum(c0, j0), 0, None).astype(np.float32) / CMP_BLOCK
    overlap[n_cmp:, :] = 0.0
    overlap[:, n_sel:] = 0.0
    onehot = (np.arange(s)[:, None] // SEL_BLOCK == np.arange(LANES)[None, :]).astype(np.float32)
    return dict(
        ncp=ncp,
        b_near=_bucket_np(d_near), ok_near=d_near >= 0,
        b_win=_bucket_np(d_win), ok_win=(d_win >= 0) & (d_win < WINDOW),
        b_cmp=_bucket_np(d_cmp), ok_cmp=(d_cmp >= 0) & (d_cmp < FAR_DIST), future_cmp=d_cmp < 0,
        overlap=overlap, onehot=onehot)


def _bias_tables(rel_bias, tabs, n_groups):
    tbl = rel_bias.T.astype(F32) * LOG2E
    reb = tbl - tbl[:, FAR_BUCKET:FAR_BUCKET + 1]
    heads = reb.shape[0]

    def toeplitz(bucket, ok, width):
        seq = jnp.where(jnp.asarray(ok)[None, :], reb[:, jnp.asarray(bucket)], -MASK_BIG)
        period = width + Q_BLOCK
        seq = jnp.pad(seq, ((0, 0), (0, period - seq.shape[1])))
        rolled = jnp.tile(seq, (1, Q_BLOCK + 1))[:, :Q_BLOCK * (period + 1)].reshape(heads, Q_BLOCK, period + 1)
        tile = rolled[:, ::-1, :width]
        tile = tile.reshape(n_groups, NSA_GQA, Q_BLOCK, width).transpose(0, 3, 1, 2)
        return tile.reshape(n_groups, width, NSA_GQA * Q_BLOCK)

    ncp = tabs["ncp"]
    win = jnp.where(jnp.asarray(tabs["ok_cmp"])[None], reb[:, jnp.asarray(tabs["b_cmp"])], 0.0)
    win = jnp.where(jnp.asarray(tabs["future_cmp"])[None], -MASK_BIG, win).transpose(0, 2, 1)
    lo = ncp + CMP_NEAR[0]
    past = jnp.zeros((heads, lo, Q_BLOCK), F32)
    future = jnp.full((heads, 2 * ncp - lo - win.shape[1], Q_BLOCK), -MASK_BIG, F32)
    cmp_tab = jnp.concatenate([past, win, future], axis=1)
    cmp_tab = cmp_tab.reshape(n_groups, NSA_GQA, 2 * ncp, Q_BLOCK).transpose(0, 2, 1, 3)
    return (toeplitz(tabs["b_near"], tabs["ok_near"], NEAR),
            toeplitz(tabs["b_win"], tabs["ok_win"], Q_BLOCK + WINDOW),
            cmp_tab.reshape(n_groups, 2 * ncp, NSA_GQA * Q_BLOCK))


def _key_blocks_t(a, g, front_blocks):
    b, s, _ = a.shape
    a = _with_ones_rows(a.reshape(b, s // Q_BLOCK, Q_BLOCK, g, HEAD_DIM).transpose(0, 3, 1, 4, 2))
    return jnp.pad(a, ((0, 0), (0, 0), (front_blocks, 0), (0, 0), (0, 0)))


def _mixer(h, w_in, f_bias, cmp_pe, cmp_w1, cmp_w2, biases, tabs, b, s):
    d = h.shape[1]
    n_heads = d // HEAD_DIM
    hn = n_heads // 2
    hf = n_heads - hn
    g = hn // NSA_GQA
    gw = g * HEAD_DIM
    qn_w, fox_w = hn * HEAD_DIM, hf * HEAD_DIM
    splits = (qn_w, gw, gw, gw, gw, gw, gw, hn * 3, fox_w, fox_w, fox_w, hf)
    offs = np.concatenate([[0], np.cumsum(splits)])
    col = lambda k: w_in[:, offs[k]:offs[k + 1]]
    scale = HEAD_DIM ** -0.5 * LOG2E

    w_main = jnp.concatenate([col(k) for k in (0, 1, 2, 3, 4, 5, 6, 8, 9, 10)], axis=1).astype(BF16)
    blocks = np.cumsum([0, hn, g, g, g, g, g, g, hf, hf, hf])
    cscale = np.ones((1, w_main.shape[1]), np.float32)
    cscale[0, :qn_w] = scale
    cscale[0, blocks[7] * HEAD_DIM:blocks[8] * HEAD_DIM] = scale
    proj = _matmul(h, w_main, cscale=jnp.asarray(cscale), out_dtype=BF16)
    proj3 = proj.reshape(b, s, -1)
    grp = lambda k: proj3[:, :, blocks[k] * HEAD_DIM:blocks[k + 1] * HEAD_DIM]

    w_gate = jnp.pad(col(7).reshape(d, g, NSA_GQA * 3), ((0, 0), (0, 0), (0, LANES - NSA_GQA * 3))).reshape(d, g * LANES)
    w_f = jnp.pad(col(11), ((0, 0), (0, LANES - hf)))
    small = _matmul(h, jnp.concatenate([w_gate, w_f], axis=1).astype(BF16), tn=(g + 1) * LANES)
    small3 = small.reshape(b, s, -1)
    gates = small3[:, :, :g * LANES]
    f_raw = small3[:, :, g * LANES:]

    cum = _forget_cumsum(f_raw, jnp.pad(f_bias.astype(F32), (0, LANES - hf)).reshape(1, LANES))
    o_fox = _fox_attention(proj3, int(blocks[7]), int(blocks[8]), grp(9).reshape(b, s, hf, HEAD_DIM),
                           cum[:, :, :hf])

    ncp = tabs["ncp"]

    def to_rows(a):
        a = a.reshape(b, s // CMP_STRIDE, CMP_STRIDE, g, HEAD_DIM).transpose(0, 3, 1, 2, 4)
        a = a.reshape(b, g, s // CMP_STRIDE, CMP_STRIDE * HEAD_DIM)
        return jnp.pad(a, ((0, 0), (0, 0), (0, ncp - s // CMP_STRIDE), (0, 0)))

    rows = jnp.stack([to_rows(grp(1)), to_rows(grp(2))])
    half = CMP_STRIDE * HEAD_DIM
    w1cat = jnp.concatenate([cmp_w1[:, :half], cmp_w1[:, half:]], axis=2).astype(BF16)
    pe_flat = jnp.pad(cmp_pe.reshape(2, 1, CMP_BLOCK * HEAD_DIM), ((0, 0), (0, 7), (0, 0))).astype(BF16)
    kv_cmp = _compress(rows, w1cat, pe_flat, cmp_w2.astype(BF16))

    bias_near, bias_win, bias_cmp = biases
    o_cmp, negsel = _cmp_attention(proj3, kv_cmp, bias_cmp, jnp.asarray(tabs["overlap"], BF16), g)

    onehot = jnp.broadcast_to(jnp.asarray(tabs["onehot"], BF16)[None, :, None, :], (b, s, g, LANES))
    k_aug = jnp.concatenate([grp(3).reshape(b, s, g, HEAD_DIM), onehot], axis=-1).reshape(b, s, g * 2 * HEAD_DIM)
    front = lambda a, n: jnp.pad(a, ((0, 0), (n, 0), (0, 0)))
    o_nsa = _sel_win_attention(proj3, front(k_aug, Q_BLOCK), _key_blocks_t(grp(4), g, 1),
                               front(grp(5), WINDOW), _key_blocks_t(grp(6), g, WINDOW // Q_BLOCK),
                               negsel, o_cmp, gates, bias_near, bias_win, g)
    return o_nsa.reshape(b * s, qn_w), o_fox.reshape(b * s, fox_w)


def _pad_to(a, axis, mult):
    n = a.shape[axis]
    target = -(-n // mult) * mult
    if target == n:
        return a
    pad = [(0, 0)] * a.ndim
    pad[axis] = (0, target - n)
    return jnp.pad(a, pad)


def _dense_ffn(x, h, wg, wu, wd):
    ff_tile = 512
    wg = _pad_to(wg, 1, ff_tile).astype(BF16)
    wu = _pad_to(wu, 1, ff_tile).astype(BF16)
    wd = _pad_to(wd, 0, ff_tile).astype(BF16)
    act = _matmul(h, wg, w2=wu, out_dtype=BF16, tn=ff_tile)
    return _matmul(act, wd, res=x, tm=1024, tn=1024, tk=wd.shape[0] // 4)


def kernel(x, attn_norm, w_in, fgate_bias, cmp_pe, cmp_w1, cmp_w2, rel_bias, out_norm_nsa, out_norm_fox,
           w_out, ffn_norm, dense_w_gate, dense_w_up, dense_w_down, router_w, moe_w_gate, moe_w_up,
           moe_w_down, final_norm):
    b, s, d = x.shape
    depth = attn_norm.shape[0]
    n_groups = (d // HEAD_DIM // 2) // NSA_GQA
    assert s % SEL_TK == 0 and s // SEL_BLOCK <= LANES and d % (2 * NSA_GQA * HEAD_DIM) == 0
    assert (b * s) % MOE_TM == 0
    tabs = _static_tables(s)
    biases = _bias_tables(rel_bias, tabs, n_groups)
    xt = x.reshape(b * s, d)
    for layer in range(depth):
        h = _rmsnorm(xt, attn_norm[layer], BF16)
        o_nsa, o_fox = _mixer(h, w_in[layer], fgate_bias[layer], cmp_pe[layer], cmp_w1[layer], cmp_w2[layer],
                              biases, tabs, b, s)
        mixed = _pair_rmsnorm(o_nsa, o_fox, out_norm_nsa[layer], out_norm_fox[layer])
        xt = _matmul(mixed, w_out[layer].astype(BF16), res=xt)
        h = _rmsnorm(xt, ffn_norm[layer], BF16)
        i = layer // 2
        if layer % 2 == 0:
            xt = _dense_ffn(xt, h, dense_w_gate[i], dense_w_up[i], dense_w_down[i])
        elif layer == depth - 1:
            return _moe_ffn(xt, h, router_w[i], moe_w_gate[i], moe_w_up[i], moe_w_down[i],
                            norm_gain=final_norm).astype(x.dtype).reshape(b, s, d)
        else:
            xt = _moe_ffn(xt, h, router_w[i], moe_w_gate[i], moe_w_up[i], moe_w_down[i])
    return _rmsnorm(xt, final_norm, x.dtype).reshape(b, s, d)
```
